```python
import math
import jax, jax.numpy as jnp
from jax import lax
import numpy as np

D_MODEL = 1024
BATCH = 32
SEQ = 2048
DEPTH = 2

N_HEADS = 8
HEAD_DIM = D_MODEL // N_HEADS
ROT_DIM = HEAD_DIM // 4
ROPE_THETA = 500000.0
D_FF = ((8 * D_MODEL // 3 + 255) // 256) * 256
MOBA_BLOCK = 256
MOBA_TOPK = 3
MOBA_Q_CHUNK = 64
SB_Q_BLOCK = 128
N_A_LAYERS = DEPTH // 2
N_B_LAYERS = DEPTH - N_A_LAYERS
DEEPNORM_ALPHA = (2.0 * DEPTH) ** 0.25
DEEPNORM_BETA = (8.0 * DEPTH) ** -0.25
LN_EPS = 1e-5
N_SUB = 3
NEG_INF = -1e30

kernel_name = 'yoco_moba_stickbreak_macaron_deepnorm_adaln'


def layer_norm(x, g, b):
    xf = x.astype(jnp.float32)
    mu = jnp.mean(xf, axis=-1, keepdims=True)
    var = jnp.mean(jnp.square(xf - mu), axis=-1, keepdims=True)
    return ((xf - mu) * lax.rsqrt(var + LN_EPS) * g.astype(jnp.float32) + b.astype(jnp.float32)).astype(x.dtype)


def modulate(x, shift, scale):
    return x * (1.0 + scale[:, None, :]) + shift[:, None, :]


def post_norm_residual(x, y, gate, weight, g, b):
    return layer_norm(DEEPNORM_ALPHA * x + weight * (1.0 + gate[:, None, :]) * y, g, b)


def swiglu(h, w_in, w_out):
    g, u = jnp.split(h @ w_in, 2, axis=-1)
    return (jax.nn.silu(g) * u) @ w_out


def split_heads(t):
    b, s, _ = t.shape
    return t.reshape(b, s, N_HEADS, HEAD_DIM).transpose(0, 2, 1, 3)


def merge_heads(t):
    b, h, s, d = t.shape
    return t.transpose(0, 2, 1, 3).reshape(b, s, h * d)


def partial_rotary(t, pos):
    half = ROT_DIM // 2
    inv_freq = jnp.power(ROPE_THETA, -jnp.arange(half, dtype=jnp.float32) * 2.0 / ROT_DIM)
    ang = pos.astype(jnp.float32)[:, None] * inv_freq[None, :]
    cos = jnp.cos(ang)[None, None]
    sin = jnp.sin(ang)[None, None]
    tr = t[..., :ROT_DIM].astype(jnp.float32)
    t1, t2 = tr[..., :half], tr[..., half:]
    rot = jnp.concatenate([t1 * cos - t2 * sin, t2 * cos + t1 * sin], axis=-1).astype(t.dtype)
    return jnp.concatenate([rot, t[..., ROT_DIM:]], axis=-1)


def _moba_one_sequence(q_b, kb_b, vb_b, idx_b, k_own_b, v_own_b, own_mask, scale):
    h, qc, n_sel = idx_b.shape
    heads = jnp.arange(h)[:, None, None]
    kg = kb_b[heads, idx_b]
    vg = vb_b[heads, idx_b]
    s_sel = jnp.einsum('hqd,hqjkd->hqjk', q_b, kg).reshape(h, qc, -1).astype(jnp.float32) * scale
    s_own = jnp.einsum('hqd,hkd->hqk', q_b, k_own_b).astype(jnp.float32) * scale
    s_own = jnp.where(own_mask[None], s_own, NEG_INF)
    p = jax.nn.softmax(jnp.concatenate([s_sel, s_own], axis=-1), axis=-1).astype(vg.dtype)
    n_k = n_sel * MOBA_BLOCK
    p_sel = p[..., :n_k].reshape(h, qc, n_sel, MOBA_BLOCK)
    p_own = p[..., n_k:]
    return (jnp.einsum('hqjk,hqjkd->hqd', p_sel, vg)
            + jnp.einsum('hqk,hkd->hqd', p_own, v_own_b))


def moba_attention(q, k, v):
    b, h, s, dh = q.shape
    scale = dh ** -0.5
    n_full = s // MOBA_BLOCK
    kbar = k[:, :, :n_full * MOBA_BLOCK].reshape(b, h, n_full, MOBA_BLOCK, dh).astype(jnp.float32).mean(axis=3)
    outs = []
    for t0 in range(0, s, MOBA_Q_CHUNK):
        cur = t0 // MOBA_BLOCK
        own_lo = cur * MOBA_BLOCK
        own_hi = min(own_lo + MOBA_BLOCK, s)
        q_c = q[:, :, t0:t0 + MOBA_Q_CHUNK]
        k_own = k[:, :, own_lo:own_hi]
        v_own = v[:, :, own_lo:own_hi]
        own_mask = (own_lo + jnp.arange(own_hi - own_lo))[None, :] <= (t0 + jnp.arange(MOBA_Q_CHUNK))[:, None]
        n_sel = min(MOBA_TOPK, cur)
        if n_sel == 0:
            sc = jnp.einsum('bhqd,bhkd->bhqk', q_c, k_own).astype(jnp.float32) * scale
            p = jax.nn.softmax(jnp.where(own_mask[None, None], sc, NEG_INF), axis=-1).astype(v.dtype)
            outs.append(jnp.einsum('bhqk,bhkd->bhqd', p, v_own))
        else:
            gate = jnp.einsum('bhqd,bhnd->bhqn', q_c.astype(jnp.float32), kbar[:, :, :cur])
            _, idx = lax.top_k(gate, n_sel)
            kb = k[:, :, :cur * MOBA_BLOCK].reshape(b, h, cur, MOBA_BLOCK, dh)
            vb = v[:, :, :cur * MOBA_BLOCK].reshape(b, h, cur, MOBA_BLOCK, dh)
            o = lax.map(lambda a: _moba_one_sequence(a[0], a[1], a[2], a[3], a[4], a[5], own_mask, scale),
                        (q_c, kb, vb, idx, k_own, v_own))
            outs.append(o)
    return jnp.concatenate(outs, axis=2)


def stick_breaking_attention(q, k, v):
    b, h, s, dh = q.shape
    scale = dh ** -0.5
    outs = []
    for t0 in range(0, s, SB_Q_BLOCK):
        t1 = t0 + SB_Q_BLOCK
        z = jnp.einsum('bhqd,bhkd->bhqk', q[:, :, t0:t1], k[:, :, :t1]).astype(jnp.float32) * scale
        strict = jnp.arange(t1)[None, :] < (t0 + jnp.arange(SB_Q_BLOCK))[:, None]
        log_keep = jnp.where(strict, jax.nn.log_sigmoid(-z), 0.0)
        log_after = lax.cumsum(log_keep, axis=3, reverse=True) - log_keep
        w = jnp.where(strict, jnp.exp(jax.nn.log_sigmoid(z) + log_after), 0.0)
        outs.append(jnp.einsum('bhqk,bhkd->bhqd', w.astype(v.dtype), v[:, :, :t1]))
    return jnp.concatenate(outs, axis=2)


def setup_inputs(seed: int = 0) -> dict:
    key = jax.random.key(seed)
    ks = jax.random.split(key, 14)
    d, f = D_MODEL, D_FF
    nrm = jax.random.normal
    return {
        'x': nrm(ks[0], (BATCH, SEQ, d), jnp.float32),
        'c': nrm(ks[1], (BATCH, d), jnp.float32),
        'w_ada': nrm(ks[2], (DEPTH, d, 3 * N_SUB * d), jnp.float32) * (0.1 * d ** -0.5),
        'b_ada': nrm(ks[3], (DEPTH, 3 * N_SUB * d), jnp.float32) * 0.01,
        'ln_g': 1.0 + 0.02 * nrm(ks[4], (DEPTH, N_SUB, d), jnp.float32),
        'ln_b': 0.02 * nrm(ks[5], (DEPTH, N_SUB, d), jnp.float32),
        'w_ffn_in': nrm(ks[6], (DEPTH, 2, d, 2 * f), jnp.float32) * d ** -0.5,
        'w_ffn_out': nrm(ks[7], (DEPTH, 2, f, d), jnp.float32) * (f ** -0.5 * DEEPNORM_BETA),
        'w_qkv_a': nrm(ks[8], (N_A_LAYERS, d, 3 * d), jnp.float32) * d ** -0.5,
        'w_q_b': nrm(ks[9], (N_B_LAYERS, d, d), jnp.float32) * d ** -0.5,
        'w_kv_ada': nrm(ks[10], (d, 2 * d), jnp.float32) * (0.1 * d ** -0.5),
        'b_kv_ada': nrm(ks[11], (2 * d,), jnp.float32) * 0.01,
        'w_kv_b': nrm(ks[12], (d, 2 * d), jnp.float32) * d ** -0.5,
        'w_o': nrm(ks[13], (DEPTH, d, d), jnp.float32) * (d ** -0.5 * DEEPNORM_BETA),
    }


def reference(x, c, w_ada, b_ada, ln_g, ln_b, w_ffn_in, w_ffn_out, w_qkv_a, w_q_b,
              w_kv_ada, b_kv_ada, w_kv_b, w_o):
    b, s, d = x.shape
    pos = jnp.arange(s)
    cond = jax.nn.silu(c)
    k_sh = None
    v_sh = None
    for l in range(DEPTH):
        if l == N_A_LAYERS:
            kv_mod = (cond @ w_kv_ada + b_kv_ada).reshape(b, 2, d)
            kv = modulate(x, kv_mod[:, 0], kv_mod[:, 1]) @ w_kv_b
            k_sh, v_sh = jnp.split(kv, 2, axis=-1)
            k_sh = split_heads(k_sh)
            v_sh = split_heads(v_sh)
        mod = (cond @ w_ada[l] + b_ada[l]).reshape(b, 3 * N_SUB, d)
        h = modulate(x, mod[:, 0], mod[:, 1])
        x = post_norm_residual(x, swiglu(h, w_ffn_in[l, 0], w_ffn_out[l, 0]), mod[:, 2], 0.5, ln_g[l, 0], ln_b[l, 0])
        h = modulate(x, mod[:, 3], mod[:, 4])
        if l < N_A_LAYERS:
            q, k, v = jnp.split(h @ w_qkv_a[l], 3, axis=-1)
            q = partial_rotary(split_heads(q), pos)
            k = partial_rotary(split_heads(k), pos)
            y = moba_attention(q, k, split_heads(v))
        else:
            q = split_heads(h @ w_q_b[l - N_A_LAYERS])
            y = stick_breaking_attention(q, k_sh, v_sh)
        y = merge_heads(y) @ w_o[l]
        x = post_norm_residual(x, y, mod[:, 5], 1.0, ln_g[l, 1], ln_b[l, 1])
        h = modulate(x, mod[:, 6], mod[:, 7])
        x = post_norm_residual(x, swiglu(h, w_ffn_in[l, 1], w_ffn_out[l, 1]), mod[:, 8], 0.5, ln_g[l, 2], ln_b[l, 2])
    return x
```

```python
import functools

import jax
import jax.numpy as jnp
from jax import lax
from jax.experimental import pallas as pl
from jax.experimental.pallas import tpu as pltpu

N_HEADS = 8
ROT_FRACTION = 4
ROPE_THETA = 500000.0
MOBA_BLOCK = 256
MOBA_TOPK = 3
SB_BLOCK = 256
LN_EPS = 1e-5
NEG_INF = -1e30
N_SUB = 3
LANES = 128

F32 = jnp.float32
BF16 = jnp.bfloat16

VMEM_LIMIT = 56 * 1024 * 1024


def _params(n_axes):
    return pltpu.CompilerParams(
        dimension_semantics=("parallel",) * n_axes, vmem_limit_bytes=VMEM_LIMIT)


def _dot(a, b):
    return jnp.dot(a, b, preferred_element_type=F32)


def _dot_t(a, b):
    return lax.dot_general(a, b, (((1,), (1,)), ((), ())), preferred_element_type=F32)


def _layer_norm(r, g, b):
    mu = jnp.mean(r, axis=-1, keepdims=True)
    d = r - mu
    var = jnp.mean(d * d, axis=-1, keepdims=True)
    return d * lax.rsqrt(var + LN_EPS) * g + b


def _ada_kernel(c_ref, w_ref, b_ref, o_ref):
    c = c_ref[...]
    cond = (c * jax.nn.sigmoid(c)).astype(BF16)
    o_ref[...] = _dot(cond, w_ref[...].astype(BF16)) + b_ref[...]


def _ada(c, w, b, tn=1024):
    nl, d, n = w.shape
    bsz = c.shape[0]
    return pl.pallas_call(
        _ada_kernel,
        grid=(nl, n // tn),
        in_specs=[
            pl.BlockSpec((bsz, d), lambda l, j: (0, 0)),
            pl.BlockSpec((None, d, tn), lambda l, j: (l, 0, j)),
            pl.BlockSpec((None, 1, tn), lambda l, j: (l, 0, j)),
        ],
        out_specs=pl.BlockSpec((None, bsz, tn), lambda l, j: (l, 0, j)),
        out_shape=jax.ShapeDtypeStruct((nl, bsz, n), F32),
        compiler_params=_params(2),
        name="ada",
    )(c, w, b.reshape(nl, 1, n))


def _ffn_kernel(x_ref, mod_ref, wg_ref, wu_ref, wo_ref, lg_ref, lb_ref, o_ref, *, k0, sub, alpha):
    x = x_ref[...]
    shift = mod_ref[0, k0:k0 + 1, :]
    scale = mod_ref[0, k0 + 1:k0 + 2, :]
    gate = mod_ref[0, k0 + 2:k0 + 3, :]
    h = (x * (1.0 + scale) + shift).astype(BF16)
    g = _dot(h, wg_ref[...])
    u = _dot(h, wu_ref[...])
    a = (g * jax.nn.sigmoid(g) * u).astype(BF16)
    y = _dot(a, wo_ref[...])
    r = alpha * x + (0.5 * (1.0 + gate)) * y
    o_ref[...] = _layer_norm(r, lg_ref[sub:sub + 1, :], lb_ref[sub:sub + 1, :])


def _ffn(x, mod_all, w_in, w_out, ln_g, ln_b, *, layer, which, seq, alpha, tm=256):
    t, d = x.shape
    f = w_out.shape[2]
    k0 = 0 if which == 0 else 6
    sub = 0 if which == 0 else 2
    per_seq = seq // tm
    kern = functools.partial(_ffn_kernel, k0=k0, sub=sub, alpha=alpha)
    return pl.pallas_call(
        kern,
        grid=(t // tm,),
        in_specs=[
            pl.BlockSpec((tm, d), lambda i: (i, 0)),
            pl.BlockSpec((None, 1, 3 * N_SUB, d), lambda i: (layer, i // per_seq, 0, 0)),
            pl.BlockSpec((None, None, d, f), lambda i: (layer, which, 0, 0)),
            pl.BlockSpec((None, None, d, f), lambda i: (layer, which, 0, 1)),
            pl.BlockSpec((None, None, f, d), lambda i: (layer, which, 0, 0)),
            pl.BlockSpec((None, N_SUB, d), lambda i: (layer, 0, 0)),
            pl.BlockSpec((None, N_SUB, d), lambda i: (layer, 0, 0)),
        ],
        out_specs=pl.BlockSpec((tm, d), lambda i: (i, 0)),
        out_shape=jax.ShapeDtypeStruct((t, d), F32),
        compiler_params=_params(1),
        name=f"ffn_l{layer}_{which}",
    )(x, mod_all, w_in, w_in, w_out, ln_g, ln_b)


def _rotary_tables(seq, head_dim):
    rot = head_dim // ROT_FRACTION
    half = rot // 2
    inv_freq = jnp.power(ROPE_THETA, -jnp.arange(half, dtype=F32) * 2.0 / rot)
    ang = jnp.arange(seq, dtype=F32)[:, None] * inv_freq[None, :]
    cos, sin = jnp.cos(ang), jnp.sin(ang)
    ones = jnp.ones((seq, head_dim - rot), F32)
    zeros = jnp.zeros((seq, head_dim - rot), F32)
    zh = jnp.zeros((seq, half), F32)
    c_tab = jnp.concatenate([cos, cos, ones], axis=1)
    up_tab = jnp.concatenate([zh, sin, zeros], axis=1)
    dn_tab = jnp.concatenate([-sin, zh, zeros], axis=1)
    return c_tab, up_tab, dn_tab


def _qkv_kernel(x_ref, mod_ref, w_ref, c_ref, up_ref, dn_ref, q_ref, k_ref, v_ref, kbar_ref, *, half):
    x = x_ref[...]
    d = x.shape[1]
    hd = c_ref.shape[1]
    shift = mod_ref[0, 3:4, :]
    scale = mod_ref[0, 4:5, :]
    h = (x * (1.0 + scale) + shift).astype(BF16)
    qkv = _dot(h, w_ref[...])
    c_tab, up_tab, dn_tab = c_ref[...], up_ref[...], dn_ref[...]

    def rotate(t):
        parts = []
        for hh in range(d // hd):
            th = t[:, hh * hd:(hh + 1) * hd]
            parts.append(th * c_tab
                         + pltpu.roll(th, half, 1) * up_tab
                         + pltpu.roll(th, hd - half, 1) * dn_tab)
        return jnp.concatenate(parts, axis=1)

    q = rotate(qkv[:, :d])
    k = rotate(qkv[:, d:2 * d])
    q_ref[...] = q.astype(q_ref.dtype)
    k_ref[...] = k.astype(k_ref.dtype)
    v_ref[...] = qkv[:, 2 * d:].astype(v_ref.dtype)
    for r in range(k.shape[0] // MOBA_BLOCK):
        kbar_ref[0, r:r + 1, :] = jnp.mean(k[r * MOBA_BLOCK:(r + 1) * MOBA_BLOCK], axis=0, keepdims=True)


def _qkv(x, mod_all, w, tabs, *, layer, seq, tm=512):
    t, d = x.shape
    hd = d // N_HEADS
    per_seq = seq // tm
    nb = tm // MOBA_BLOCK
    kern = functools.partial(_qkv_kernel, half=hd // ROT_FRACTION // 2)
    tab_spec = pl.BlockSpec((tm, hd), lambda i: (i % per_seq, 0))
    tok_spec = pl.BlockSpec((tm, d), lambda i: (i, 0))
    return pl.pallas_call(
        kern,
        grid=(t // tm,),
        in_specs=[
            tok_spec,
            pl.BlockSpec((None, 1, 3 * N_SUB, d), lambda i: (layer, i // per_seq, 0, 0)),
            pl.BlockSpec((None, d, 3 * d), lambda i: (layer, 0, 0)),
            tab_spec, tab_spec, tab_spec,
        ],
        out_specs=[tok_spec, tok_spec, tok_spec,
                   pl.BlockSpec((1, nb, d), lambda i: (i, 0, 0))],
        out_shape=[jax.ShapeDtypeStruct((t, d), BF16)] * 3
        + [jax.ShapeDtypeStruct((t // tm, nb, d), F32)],
        compiler_params=_params(1),
        name=f"qkv_l{layer}",
    )(x, mod_all, w, *tabs)


def _proj_kernel(x_ref, mod_ref, w_ref, o_ref, *, r_shift, r_scale):
    x = x_ref[...]
    shift = mod_ref[0, r_shift:r_shift + 1, :]
    scale = mod_ref[0, r_scale:r_scale + 1, :]
    h = (x * (1.0 + scale) + shift).astype(BF16)
    o_ref[...] = _dot(h, w_ref[...]).astype(o_ref.dtype)


def _proj(x, mod, w, *, mod_layer, w_layer, r_shift, r_scale, seq, tm=512, name="proj"):
    t, d = x.shape
    n = w.shape[2]
    rows = mod.shape[2]
    per_seq = seq // tm
    kern = functools.partial(_proj_kernel, r_shift=r_shift, r_scale=r_scale)
    return pl.pallas_call(
        kern,
        grid=(t // tm,),
        in_specs=[
            pl.BlockSpec((tm, d), lambda i: (i, 0)),
            pl.BlockSpec((None, 1, rows, d), lambda i: (mod_layer, i // per_seq, 0, 0)),
            pl.BlockSpec((None, d, n), lambda i: (w_layer, 0, 0)),
        ],
        out_specs=pl.BlockSpec((tm, n), lambda i: (i, 0)),
        out_shape=jax.ShapeDtypeStruct((t, n), BF16),
        compiler_params=_params(1),
        name=name,
    )(x, mod, w)


def _oproj_kernel(a_ref, x_ref, mod_ref, w_ref, lg_ref, lb_ref, o_ref, *, alpha):
    x = x_ref[...]
    gate = mod_ref[0, 5:6, :]
    y = _dot(a_ref[...], w_ref[...])
    r = alpha * x + (1.0 + gate) * y
    o_ref[...] = _layer_norm(r, lg_ref[1:2, :], lb_ref[1:2, :])


def _oproj(att, x, mod_all, w, ln_g, ln_b, *, layer, seq, alpha, tm=512):
    t, d = x.shape
    per_seq = seq // tm
    kern = functools.partial(_oproj_kernel, alpha=alpha)
    tok_spec = pl.BlockSpec((tm, d), lambda i: (i, 0))
    return pl.pallas_call(
        kern,
        grid=(t // tm,),
        in_specs=[
            tok_spec, tok_spec,
            pl.BlockSpec((None, 1, 3 * N_SUB, d), lambda i: (layer, i // per_seq, 0, 0)),
            pl.BlockSpec((None, d, d), lambda i: (layer, 0, 0)),
            pl.BlockSpec((None, N_SUB, d), lambda i: (layer, 0, 0)),
            pl.BlockSpec((None, N_SUB, d), lambda i: (layer, 0, 0)),
        ],
        out_specs=tok_spec,
        out_shape=jax.ShapeDtypeStruct((t, d), F32),
        compiler_params=_params(1),
        name=f"oproj_l{layer}",
    )(att, x, mod_all, w, ln_g, ln_b)


def _moba_kernel(q_ref, k_ref, v_ref, kbar_ref, o_ref, *, nblk, scale):
    blk = MOBA_BLOCK
    hd = q_ref.shape[1]
    kb = kbar_ref[0]
    kb_hi = kb.astype(BF16)
    kb_lo = (kb - kb_hi.astype(F32)).astype(BF16)
    pad = jnp.zeros((LANES - nblk, hd), BF16)
    kb_hi = jnp.concatenate([kb_hi, pad], axis=0)
    kb_lo = jnp.concatenate([kb_lo, pad], axis=0)
    row = lax.broadcasted_iota(jnp.int32, (blk, blk), 0)
    col = lax.broadcasted_iota(jnp.int32, (blk, blk), 1)
    lane = lax.broadcasted_iota(jnp.int32, (blk, LANES), 1)

    def q_block(i, _):
        off = pl.multiple_of(i * blk, blk)
        qi = q_ref[pl.ds(off, blk), :]
        s = _dot_t(qi, k_ref[pl.ds(off, blk), :]) * scale
        s = jnp.where(col <= row, s, NEG_INF)
        m = jnp.max(s, axis=1, keepdims=True)
        p = jnp.exp(s - m)
        l = jnp.sum(p, axis=1, keepdims=True)
        acc = _dot(p.astype(BF16), v_ref[pl.ds(off, blk), :])

        gate = _dot_t(qi, kb_hi) + _dot_t(qi, kb_lo)
        gate = jnp.where(lane < i, gate, -jnp.inf)
        sel = jnp.full((blk, LANES), NEG_INF, F32)
        for _ in range(MOBA_TOPK):
            top = jnp.max(gate, axis=1, keepdims=True)
            first = jnp.min(jnp.where(gate == top, lane, LANES), axis=1, keepdims=True)
            hit = lane == first
            sel = jnp.where(hit, 0.0, sel)
            gate = jnp.where(hit, -jnp.inf, gate)

        def kv_block(j, carry):
            m, l, acc = carry
            offj = pl.multiple_of(j * blk, blk)
            s = _dot_t(qi, k_ref[pl.ds(offj, blk), :]) * scale
            bias = jnp.max(jnp.where(lane == j, sel, NEG_INF), axis=1, keepdims=True)
            s = jnp.where(bias < 0.0, NEG_INF, s)
            m_new = jnp.maximum(m, jnp.max(s, axis=1, keepdims=True))
            a = jnp.exp(m - m_new)
            p = jnp.exp(s - m_new)
            l = a * l + jnp.sum(p, axis=1, keepdims=True)
            acc = a * acc + _dot(p.astype(BF16), v_ref[pl.ds(offj, blk), :])
            return m_new, l, acc

        m, l, acc = lax.fori_loop(0, i, kv_block, (m, l, acc))
        o_ref[pl.ds(off, blk), :] = (acc / l).astype(o_ref.dtype)
        return 0

    lax.fori_loop(0, nblk, q_block, 0)


def _moba(q, k, v, kbar, *, batch, seq):
    t, d = q.shape
    hd = d // N_HEADS
    nblk = seq // MOBA_BLOCK
    kern = functools.partial(_moba_kernel, nblk=nblk, scale=hd ** -0.5)
    head_spec = pl.BlockSpec((seq, hd), lambda b, h: (b, h))
    return pl.pallas_call(
        kern,
        grid=(batch, N_HEADS),
        in_specs=[head_spec, head_spec, head_spec,
                  pl.BlockSpec((1, nblk, hd), lambda b, h: (b, 0, h))],
        out_specs=head_spec,
        out_shape=jax.ShapeDtypeStruct((t, d), BF16),
        compiler_params=_params(2),
        name="moba",
    )(q, k, v, kbar)


def _softplus(z):
    return jnp.maximum(z, 0.0) + jnp.log1p(jnp.exp(-jnp.abs(z)))


def _sb_kernel(q_ref, k_ref, v_ref, o_ref, *, nblk, scale):
    blk = SB_BLOCK
    row = lax.broadcasted_iota(jnp.int32, (blk, blk), 0)
    col = lax.broadcasted_iota(jnp.int32, (blk, blk), 1)
    strict = col < row
    after = (row > col).astype(BF16)

    def sum_after(lk):
        hi = lk.astype(BF16)
        lo = (lk - hi.astype(F32)).astype(BF16)
        return _dot(hi, after) + _dot(lo, after)

    def q_block(i, _):
        off = pl.multiple_of(i * blk, blk)
        qi = q_ref[pl.ds(off, blk), :]
        z = _dot_t(qi, k_ref[pl.ds(off, blk), :]) * scale
        sp = _softplus(z)
        lk = jnp.where(strict, -sp, 0.0)
        w = jnp.where(strict, jnp.exp(z - sp + sum_after(lk)), 0.0)
        acc = _dot(w.astype(BF16), v_ref[pl.ds(off, blk), :])
        tail = jnp.sum(lk, axis=1, keepdims=True)

        def kv_block(jj, carry):
            acc, tail = carry
            offj = pl.multiple_of((i - 1 - jj) * blk, blk)
            z = _dot_t(qi, k_ref[pl.ds(offj, blk), :]) * scale
            sp = _softplus(z)
            lk = -sp
            w = jnp.exp(z - sp + (sum_after(lk) + tail))
            acc = acc + _dot(w.astype(BF16), v_ref[pl.ds(offj, blk), :])
            return acc, tail + jnp.sum(lk, axis=1, keepdims=True)

        acc, _ = lax.fori_loop(0, i, kv_block, (acc, tail))
        o_ref[pl.ds(off, blk), :] = acc.astype(o_ref.dtype)
        return 0

    lax.fori_loop(0, nblk, q_block, 0)


def _sb(q, kv, *, batch, seq):
    t, d = q.shape
    hd = d // N_HEADS
    nblk = seq // SB_BLOCK
    kern = functools.partial(_sb_kernel, nblk=nblk, scale=hd ** -0.5)
    return pl.pallas_call(
        kern,
        grid=(batch, N_HEADS),
        in_specs=[pl.BlockSpec((seq, hd), lambda b, h: (b, h)),
                  pl.BlockSpec((seq, hd), lambda b, h: (b, h)),
                  pl.BlockSpec((seq, hd), lambda b, h: (b, N_HEADS + h))],
        out_specs=pl.BlockSpec((seq, hd), lambda b, h: (b, h)),
        out_shape=jax.ShapeDtypeStruct((t, d), BF16),
        compiler_params=_params(2),
        name="stickbreak",
    )(q, kv, kv)


def kernel(x, c, w_ada, b_ada, ln_g, ln_b, w_ffn_in, w_ffn_out, w_qkv_a, w_q_b, w_kv_ada, b_kv_ada, w_kv_b, w_o):
    bsz, seq, d = x.shape
    depth = w_ada.shape[0]
    n_a = w_qkv_a.shape[0]
    alpha = (2.0 * depth) ** 0.25
    t = bsz * seq
    assert seq % MOBA_BLOCK == 0 and seq % SB_BLOCK == 0 and d % (N_HEADS * LANES) == 0

    mod_all = _ada(c, w_ada, b_ada).reshape(depth, bsz, 3 * N_SUB, d)
    kv_mod = _ada(c, w_kv_ada[None], b_kv_ada[None]).reshape(1, bsz, 2, d)

    w_in = w_ffn_in.astype(BF16)
    w_out = w_ffn_out.astype(BF16)
    w_qkv = w_qkv_a.astype(BF16)
    w_q = w_q_b.astype(BF16)
    w_kv = w_kv_b.astype(BF16)[None]
    w_ob = w_o.astype(BF16)
    tabs = _rotary_tables(seq, d // N_HEADS)

    xf = x.reshape(t, d)
    kv = None
    for l in range(depth):
        if l == n_a:
            kv = _proj(xf, kv_mod, w_kv, mod_layer=0, w_layer=0, r_shift=0, r_scale=1, seq=seq, name="kv_proj")
        xf = _ffn(xf, mod_all, w_in, w_out, ln_g, ln_b, layer=l, which=0, seq=seq, alpha=alpha)
        if l < n_a:
            q, k, v, kbar = _qkv(xf, mod_all, w_qkv, tabs, layer=l, seq=seq)
            att = _moba(q, k, v, kbar.reshape(bsz, seq // MOBA_BLOCK, d), batch=bsz, seq=seq)
        else:
            q = _proj(xf, mod_all, w_q, mod_layer=l, w_layer=l - n_a, r_shift=3, r_scale=4, seq=seq,
                      name=f"q_proj_l{l}")
            att = _sb(q, kv, batch=bsz, seq=seq)
        xf = _oproj(att, xf, mod_all, w_ob, ln_g, ln_b, layer=l, seq=seq, alpha=alpha)
        xf = _ffn(xf, mod_all, w_in, w_out, ln_g, ln_b, layer=l, which=1, seq=seq, alpha=alpha)
    return xf.reshape(bsz, seq, d)
```

```python
import functools

import jax
import jax.numpy as jnp
from jax import lax
from jax.experimental import pallas as pl
from jax.experimental.pallas import tpu as pltpu

N_HEADS = 8
ROT_FRACTION = 4
ROPE_THETA = 500000.0
MOBA_BLOCK = 256
MOBA_TOPK = 3
SB_BLOCK = 256
LN_EPS = 1e-5
NEG_INF = -1e30
N_SUB = 3
LANES = 128
BF16_SUBLANES = 16
LOG2E = 1.4426950408889634

F32 = jnp.float32
BF16 = jnp.bfloat16

VMEM_LIMIT = 56 * 1024 * 1024


def _params(n_axes):
    return pltpu.CompilerParams(
        dimension_semantics=("parallel",) * n_axes, vmem_limit_bytes=VMEM_LIMIT)


def _dot(a, b):
    return jnp.dot(a, b, preferred_element_type=F32)


def _dot_t(a, b):
    return lax.dot_general(a, b, (((1,), (1,)), ((), ())), preferred_element_type=F32)


def _dot_tn(a, b):
    return lax.dot_general(a, b, (((0,), (0,)), ((), ())), preferred_element_type=F32)


def _layer_norm(r, g, b):
    mu = jnp.mean(r, axis=-1, keepdims=True)
    d = r - mu
    var = jnp.mean(d * d, axis=-1, keepdims=True)
    return d * lax.rsqrt(var + LN_EPS) * g + b


def _ada_kernel(c_ref, w_ref, b_ref, o_ref):
    c = c_ref[...]
    cond = (c * jax.nn.sigmoid(c)).astype(BF16)
    o_ref[...] = _dot(cond, w_ref[...].astype(BF16)) + b_ref[...]


def _ada(c, w, b, tn=1024):
    nl, d, n = w.shape
    bsz = c.shape[0]
    return pl.pallas_call(
        _ada_kernel,
        grid=(nl, n // tn),
        in_specs=[
            pl.BlockSpec((bsz, d), lambda l, j: (0, 0)),
            pl.BlockSpec((None, d, tn), lambda l, j: (l, 0, j)),
            pl.BlockSpec((None, 1, tn), lambda l, j: (l, 0, j)),
        ],
        out_specs=pl.BlockSpec((None, bsz, tn), lambda l, j: (l, 0, j)),
        out_shape=jax.ShapeDtypeStruct((nl, bsz, n), F32),
        compiler_params=_params(2),
        name="ada",
    )(c, w, b.reshape(nl, 1, n))


def _ffn_kernel(x_ref, mod_ref, wg_ref, wu_ref, wo_ref, lg_ref, lb_ref, o_ref, *, k0, sub, alpha, rows):
    shift = mod_ref[0, k0:k0 + 1, :]
    scale = mod_ref[0, k0 + 1:k0 + 2, :]
    gate = mod_ref[0, k0 + 2:k0 + 3, :]
    for r0 in range(0, x_ref.shape[0], rows):
        x = x_ref[r0:r0 + rows, :]
        h = (x * (1.0 + scale) + shift).astype(BF16)
        g = _dot(h, wg_ref[...])
        u = _dot(h, wu_ref[...])
        a = (g * jax.nn.sigmoid(g) * u).astype(BF16)
        y = _dot(a, wo_ref[...])
        r = alpha * x + (0.5 * (1.0 + gate)) * y
        o_ref[r0:r0 + rows, :] = _layer_norm(r, lg_ref[sub:sub + 1, :], lb_ref[sub:sub + 1, :])


def _ffn(x, mod_all, w_in, w_out, ln_g, ln_b, *, layer, which, seq, alpha, tm=1024, rows=256):
    t, d = x.shape
    f = w_out.shape[2]
    k0 = 0 if which == 0 else 6
    sub = 0 if which == 0 else 2
    per_seq = seq // tm
    kern = functools.partial(_ffn_kernel, k0=k0, sub=sub, alpha=alpha, rows=rows)
    return pl.pallas_call(
        kern,
        grid=(t // tm,),
        in_specs=[
            pl.BlockSpec((tm, d), lambda i: (i, 0)),
            pl.BlockSpec((None, 1, 3 * N_SUB, d), lambda i: (layer, i // per_seq, 0, 0)),
            pl.BlockSpec((None, None, d, f), lambda i: (layer, which, 0, 0)),
            pl.BlockSpec((None, None, d, f), lambda i: (layer, which, 0, 1)),
            pl.BlockSpec((None, None, f, d), lambda i: (layer, which, 0, 0)),
            pl.BlockSpec((None, N_SUB, d), lambda i: (layer, 0, 0)),
            pl.BlockSpec((None, N_SUB, d), lambda i: (layer, 0, 0)),
        ],
        out_specs=pl.BlockSpec((tm, d), lambda i: (i, 0)),
        out_shape=jax.ShapeDtypeStruct((t, d), F32),
        compiler_params=_params(1),
        name=f"ffn_l{layer}_{which}",
    )(x, mod_all, w_in, w_in, w_out, ln_g, ln_b)


def _rotary_tables(seq, head_dim):
    rot = head_dim // ROT_FRACTION
    half = rot // 2
    inv_freq = jnp.power(ROPE_THETA, -jnp.arange(half, dtype=F32) * 2.0 / rot)
    ang = jnp.arange(seq, dtype=F32)[:, None] * inv_freq[None, :]
    cos, sin = jnp.cos(ang), jnp.sin(ang)
    ones = jnp.ones((seq, head_dim - rot), F32)
    zeros = jnp.zeros((seq, head_dim - rot), F32)
    zh = jnp.zeros((seq, half), F32)
    c_tab = jnp.concatenate([cos, cos, ones], axis=1)
    up_tab = jnp.concatenate([zh, sin, zeros], axis=1)
    dn_tab = jnp.concatenate([-sin, zh, zeros], axis=1)
    return c_tab, up_tab, dn_tab


def _qkv_kernel(x_ref, mod_ref, w_ref, c_ref, up_ref, dn_ref, q_ref, k_ref, v_ref, kbar_ref, *, half):
    x = x_ref[...]
    d = x.shape[1]
    hd = c_ref.shape[1]
    shift = mod_ref[0, 3:4, :]
    scale = mod_ref[0, 4:5, :]
    h = (x * (1.0 + scale) + shift).astype(BF16)
    qkv = _dot(h, w_ref[...])
    c_tab, up_tab, dn_tab = c_ref[...], up_ref[...], dn_ref[...]

    def rotate(t):
        parts = []
        for hh in range(d // hd):
            th = t[:, hh * hd:(hh + 1) * hd]
            parts.append(th * c_tab
                         + pltpu.roll(th, half, 1) * up_tab
                         + pltpu.roll(th, hd - half, 1) * dn_tab)
        return jnp.concatenate(parts, axis=1)

    q = rotate(qkv[:, :d])
    k = rotate(qkv[:, d:2 * d])
    q_ref[...] = q.astype(q_ref.dtype)
    k_ref[...] = k.astype(k_ref.dtype)
    v_ref[...] = qkv[:, 2 * d:].astype(v_ref.dtype)
    for r in range(k.shape[0] // MOBA_BLOCK):
        kbar_ref[0, r:r + 1, :] = jnp.mean(k[r * MOBA_BLOCK:(r + 1) * MOBA_BLOCK], axis=0, keepdims=True)


def _qkv(x, mod_all, w, tabs, *, layer, seq, tm=512):
    t, d = x.shape
    hd = d // N_HEADS
    per_seq = seq // tm
    nb = tm // MOBA_BLOCK
    kern = functools.partial(_qkv_kernel, half=hd // ROT_FRACTION // 2)
    tab_spec = pl.BlockSpec((tm, hd), lambda i: (i % per_seq, 0))
    tok_spec = pl.BlockSpec((tm, d), lambda i: (i, 0))
    return pl.pallas_call(
        kern,
        grid=(t // tm,),
        in_specs=[
            tok_spec,
            pl.BlockSpec((None, 1, 3 * N_SUB, d), lambda i: (layer, i // per_seq, 0, 0)),
            pl.BlockSpec((None, d, 3 * d), lambda i: (layer, 0, 0)),
            tab_spec, tab_spec, tab_spec,
        ],
        out_specs=[tok_spec, tok_spec, tok_spec,
                   pl.BlockSpec((1, nb, d), lambda i: (i, 0, 0))],
        out_shape=[jax.ShapeDtypeStruct((t, d), BF16)] * 3
        + [jax.ShapeDtypeStruct((t // tm, nb, d), F32)],
        compiler_params=_params(1),
        name=f"qkv_l{layer}",
    )(x, mod_all, w, *tabs)


def _proj_kernel(x_ref, mod_ref, w_ref, o_ref, *, r_shift, r_scale):
    x = x_ref[...]
    shift = mod_ref[0, r_shift:r_shift + 1, :]
    scale = mod_ref[0, r_scale:r_scale + 1, :]
    h = (x * (1.0 + scale) + shift).astype(BF16)
    o_ref[...] = _dot(h, w_ref[...]).astype(o_ref.dtype)


def _proj(x, mod, w, *, mod_layer, w_layer, r_shift, r_scale, seq, tm=512, name="proj"):
    t, d = x.shape
    n = w.shape[2]
    rows = mod.shape[2]
    per_seq = seq // tm
    kern = functools.partial(_proj_kernel, r_shift=r_shift, r_scale=r_scale)
    return pl.pallas_call(
        kern,
        grid=(t // tm,),
        in_specs=[
            pl.BlockSpec((tm, d), lambda i: (i, 0)),
            pl.BlockSpec((None, 1, rows, d), lambda i: (mod_layer, i // per_seq, 0, 0)),
            pl.BlockSpec((None, d, n), lambda i: (w_layer, 0, 0)),
        ],
        out_specs=pl.BlockSpec((tm, n), lambda i: (i, 0)),
        out_shape=jax.ShapeDtypeStruct((t, n), BF16),
        compiler_params=_params(1),
        name=name,
    )(x, mod, w)


def _oproj_kernel(a_ref, x_ref, mod_ref, w_ref, lg_ref, lb_ref, o_ref, *, alpha):
    x = x_ref[...]
    gate = mod_ref[0, 5:6, :]
    y = _dot(a_ref[...], w_ref[...])
    r = alpha * x + (1.0 + gate) * y
    o_ref[...] = _layer_norm(r, lg_ref[1:2, :], lb_ref[1:2, :])


def _oproj(att, x, mod_all, w, ln_g, ln_b, *, layer, seq, alpha, tm=512):
    t, d = x.shape
    per_seq = seq // tm
    kern = functools.partial(_oproj_kernel, alpha=alpha)
    tok_spec = pl.BlockSpec((tm, d), lambda i: (i, 0))
    return pl.pallas_call(
        kern,
        grid=(t // tm,),
        in_specs=[
            tok_spec, tok_spec,
            pl.BlockSpec((None, 1, 3 * N_SUB, d), lambda i: (layer, i // per_seq, 0, 0)),
            pl.BlockSpec((None, d, d), lambda i: (layer, 0, 0)),
            pl.BlockSpec((None, N_SUB, d), lambda i: (layer, 0, 0)),
            pl.BlockSpec((None, N_SUB, d), lambda i: (layer, 0, 0)),
        ],
        out_specs=tok_spec,
        out_shape=jax.ShapeDtypeStruct((t, d), F32),
        compiler_params=_params(1),
        name=f"oproj_l{layer}",
    )(att, x, mod_all, w, ln_g, ln_b)


def _moba_kernel(q_ref, k_ref, v_ref, kbar_ref, o_ref, vt_ref, s_ref, p_ref, *, nblk, scale):
    blk = MOBA_BLOCK
    hd = q_ref.shape[1]
    grows = -(-nblk // BF16_SUBLANES) * BF16_SUBLANES
    kb = jnp.concatenate([kbar_ref[0], jnp.zeros((grows - nblk, hd), F32)], axis=0)
    kb_hi = kb.astype(BF16)
    kb_lo = (kb - kb_hi.astype(F32)).astype(BF16)
    key = lax.broadcasted_iota(jnp.int32, (blk, blk), 0)
    qry = lax.broadcasted_iota(jnp.int32, (blk, blk), 1)
    causal = key <= qry
    grow = lax.broadcasted_iota(jnp.int32, (grows, blk), 0).astype(F32)
    c2 = scale * LOG2E

    for j in range(nblk):
        vt_ref[:, j * blk:(j + 1) * blk] = v_ref[j * blk:(j + 1) * blk, :].astype(F32).T.astype(BF16)

    for i in range(nblk):
        nk = (i + 1) * blk
        qi = q_ref[i * blk:nk, :]
        s_ref[0:nk, :] = _dot_t(k_ref[0:nk, :], qi) * c2

        if i > MOBA_TOPK:
            gate = _dot_t(kb_hi, qi) + _dot_t(kb_lo, qi)
            gate = jnp.where(grow < float(i), gate, -jnp.inf)
            sel = jnp.full((grows, blk), NEG_INF, F32)
            for _ in range(MOBA_TOPK):
                top = jnp.max(gate, axis=0, keepdims=True)
                first = jnp.min(jnp.where(gate == top, grow, float(grows)), axis=0, keepdims=True)
                hit = grow == first
                sel = jnp.where(hit, 0.0, sel)
                gate = jnp.where(hit, -jnp.inf, gate)

        s = jnp.where(causal, s_ref[i * blk:nk, :], NEG_INF)
        s_ref[i * blk:nk, :] = s
        m = jnp.max(s, axis=0, keepdims=True)
        for j in range(i):
            if i > MOBA_TOPK:
                s = s_ref[j * blk:(j + 1) * blk, :] + sel[j:j + 1, :]
                s_ref[j * blk:(j + 1) * blk, :] = s
            else:
                s = s_ref[j * blk:(j + 1) * blk, :]
            m = jnp.maximum(m, jnp.max(s, axis=0, keepdims=True))
        l = jnp.zeros((1, blk), F32)
        for j in range(i + 1):
            p = jnp.exp2(s_ref[j * blk:(j + 1) * blk, :] - m)
            l = l + jnp.sum(p, axis=0, keepdims=True)
            p_ref[j * blk:(j + 1) * blk, :] = p.astype(BF16)
        acc = _dot(vt_ref[:, 0:nk], p_ref[0:nk, :])
        o_ref[i * blk:nk, :] = (acc * (1.0 / l)).T.astype(o_ref.dtype)


def _moba(q, k, v, kbar, *, batch, seq):
    t, d = q.shape
    hd = d // N_HEADS
    nblk = seq // MOBA_BLOCK
    kern = functools.partial(_moba_kernel, nblk=nblk, scale=hd ** -0.5)
    head_spec = pl.BlockSpec((seq, hd), lambda b, h: (b, h))
    return pl.pallas_call(
        kern,
        grid=(batch, N_HEADS),
        in_specs=[head_spec, head_spec, head_spec,
                  pl.BlockSpec((1, nblk, hd), lambda b, h: (b, 0, h))],
        out_specs=head_spec,
        out_shape=jax.ShapeDtypeStruct((t, d), BF16),
        scratch_shapes=[pltpu.VMEM((hd, seq), BF16),
                        pltpu.VMEM((seq, MOBA_BLOCK), F32),
                        pltpu.VMEM((seq, MOBA_BLOCK), BF16)],
        compiler_params=_params(2),
        name="moba",
    )(q, k, v, kbar)


def _softplus2(z2):
    return jnp.maximum(z2, 0.0) + jnp.log2(1.0 + jnp.exp2(-jnp.abs(z2)))


def _sb_kernel(q_ref, k_ref, v_ref, o_ref, vt_ref, z_ref, w_ref, *, nblk, scale):
    blk = SB_BLOCK
    key = lax.broadcasted_iota(jnp.int32, (blk, blk), 0)
    qry = lax.broadcasted_iota(jnp.int32, (blk, blk), 1)
    strict = key < qry
    later = (qry > key).astype(BF16)
    c2 = scale * LOG2E

    def sum_later(sp):
        hi = sp.astype(BF16)
        lo = (sp - hi.astype(F32)).astype(BF16)
        return _dot(later, hi) + _dot(later, lo)

    for j in range(nblk):
        vt_ref[:, j * blk:(j + 1) * blk] = v_ref[j * blk:(j + 1) * blk, :].astype(F32).T.astype(BF16)

    for i in range(nblk):
        nk = (i + 1) * blk
        qi = q_ref[i * blk:nk, :]
        z_ref[0:nk, :] = _dot_t(k_ref[0:nk, :], qi) * c2
        tail = None
        for j in range(i, -1, -1):
            z = z_ref[j * blk:(j + 1) * blk, :]
            sp = _softplus2(z)
            if j == i:
                sp = jnp.where(strict, sp, 0.0)
            e = z - sp - sum_later(sp)
            w = jnp.exp2(e if tail is None else e - tail)
            if j == i:
                w = jnp.where(strict, w, 0.0)
            w_ref[j * blk:(j + 1) * blk, :] = w.astype(BF16)
            if j > 0:
                part = jnp.sum(sp, axis=0, keepdims=True)
                tail = part if tail is None else tail + part
        acc = _dot(vt_ref[:, 0:nk], w_ref[0:nk, :])
        o_ref[i * blk:nk, :] = acc.T.astype(o_ref.dtype)


def _sb(q, kv, *, batch, seq):
    t, d = q.shape
    hd = d // N_HEADS
    nblk = seq // SB_BLOCK
    kern = functools.partial(_sb_kernel, nblk=nblk, scale=hd ** -0.5)
    return pl.pallas_call(
        kern,
        grid=(batch, N_HEADS),
        in_specs=[pl.BlockSpec((seq, hd), lambda b, h: (b, h)),
                  pl.BlockSpec((seq, hd), lambda b, h: (b, h)),
                  pl.BlockSpec((seq, hd), lambda b, h: (b, N_HEADS + h))],
        out_specs=pl.BlockSpec((seq, hd), lambda b, h: (b, h)),
        out_shape=jax.ShapeDtypeStruct((t, d), BF16),
        scratch_shapes=[pltpu.VMEM((hd, seq), BF16),
                        pltpu.VMEM((seq, SB_BLOCK), F32),
                        pltpu.VMEM((seq, SB_BLOCK), BF16)],
        compiler_params=_params(2),
        name="stickbreak",
    )(q, kv, kv)


def kernel(x, c, w_ada, b_ada, ln_g, ln_b, w_ffn_in, w_ffn_out, w_qkv_a, w_q_b, w_kv_ada, b_kv_ada, w_kv_b, w_o):
    bsz, seq, d = x.shape
    depth = w_ada.shape[0]
    n_a = w_qkv_a.shape[0]
    alpha = (2.0 * depth) ** 0.25
    t = bsz * seq
    assert seq % MOBA_BLOCK == 0 and seq % SB_BLOCK == 0 and d % (N_HEADS * LANES) == 0

    mod_all = _ada(c, w_ada, b_ada).reshape(depth, bsz, 3 * N_SUB, d)
    kv_mod = _ada(c, w_kv_ada[None], b_kv_ada[None]).reshape(1, bsz, 2, d)

    w_in = w_ffn_in.astype(BF16)
    w_out = w_ffn_out.astype(BF16)
    w_qkv = w_qkv_a.astype(BF16)
    w_q = w_q_b.astype(BF16)
    w_kv = w_kv_b.astype(BF16)[None]
    w_ob = w_o.astype(BF16)
    tabs = _rotary_tables(seq, d // N_HEADS)

    xf = x.reshape(t, d)
    kv = None
    for l in range(depth):
        if l == n_a:
            kv = _proj(xf, kv_mod, w_kv, mod_layer=0, w_layer=0, r_shift=0, r_scale=1, seq=seq, name="kv_proj")
        xf = _ffn(xf, mod_all, w_in, w_out, ln_g, ln_b, layer=l, which=0, seq=seq, alpha=alpha)
        if l < n_a:
            q, k, v, kbar = _qkv(xf, mod_all, w_qkv, tabs, layer=l, seq=seq)
            att = _moba(q, k, v, kbar.reshape(bsz, seq // MOBA_BLOCK, d), batch=bsz, seq=seq)
        else:
            q = _proj(xf, mod_all, w_q, mod_layer=l, w_layer=l - n_a, r_shift=3, r_scale=4, seq=seq,
                      name=f"q_proj_l{l}")
            att = _sb(q, kv, batch=bsz, seq=seq)
        xf = _oproj(att, xf, mod_all, w_ob, ln_g, ln_b, layer=l, seq=seq, alpha=alpha)
        xf = _ffn(xf, mod_all, w_in, w_out, ln_g, ln_b, layer=l, which=1, seq=seq, alpha=alpha)
    return xf.reshape(bsz, seq, d)
```

```python
import functools

import jax
import jax.numpy as jnp
from jax import lax
from jax.experimental import pallas as pl
from jax.experimental.pallas import tpu as pltpu

N_HEADS = 8
ROT_FRACTION = 4
ROPE_THETA = 500000.0
MOBA_BLOCK = 256
MOBA_TOPK = 3
SB_BLOCK = 256
LN_EPS = 1e-5
NEG_INF = -1e30
N_SUB = 3
LANES = 128
BF16_SUBLANES = 16
LOG2E = 1.4426950408889634

F32 = jnp.float32
BF16 = jnp.bfloat16

VMEM_LIMIT = 56 * 1024 * 1024


def _params(n_axes):
    return pltpu.CompilerParams(
        dimension_semantics=("parallel",) * n_axes, vmem_limit_bytes=VMEM_LIMIT)


def _dot(a, b):
    return jnp.dot(a, b, preferred_element_type=F32)


def _dot_t(a, b):
    return lax.dot_general(a, b, (((1,), (1,)), ((), ())), preferred_element_type=F32)


def _dot_tn(a, b):
    return lax.dot_general(a, b, (((0,), (0,)), ((), ())), preferred_element_type=F32)


def _layer_norm(r, g, b):
    mu = jnp.mean(r, axis=-1, keepdims=True)
    d = r - mu
    var = jnp.mean(d * d, axis=-1, keepdims=True)
    return d * lax.rsqrt(var + LN_EPS) * g + b


def _ada_kernel(c_ref, w_ref, b_ref, o_ref):
    c = c_ref[...]
    cond = (c * jax.nn.sigmoid(c)).astype(BF16)
    o_ref[...] = _dot(cond, w_ref[...].astype(BF16)) + b_ref[...]


def _ada(c, w, b, tn=1024):
    nl, d, n = w.shape
    bsz = c.shape[0]
    return pl.pallas_call(
        _ada_kernel,
        grid=(nl, n // tn),
        in_specs=[
            pl.BlockSpec((bsz, d), lambda l, j: (0, 0)),
            pl.BlockSpec((None, d, tn), lambda l, j: (l, 0, j)),
            pl.BlockSpec((None, 1, tn), lambda l, j: (l, 0, j)),
        ],
        out_specs=pl.BlockSpec((None, bsz, tn), lambda l, j: (l, 0, j)),
        out_shape=jax.ShapeDtypeStruct((nl, bsz, n), F32),
        compiler_params=_params(2),
        name="ada",
    )(c, w, b.reshape(nl, 1, n))


def _ffn_kernel(x_ref, mod_ref, wg_ref, wu_ref, wo_ref, lg_ref, lb_ref, o_ref, *, k0, sub, alpha, rows):
    shift = mod_ref[0, k0:k0 + 1, :]
    scale = mod_ref[0, k0 + 1:k0 + 2, :]
    gate = mod_ref[0, k0 + 2:k0 + 3, :]
    for r0 in range(0, x_ref.shape[0], rows):
        x = x_ref[r0:r0 + rows, :]
        h = (x * (1.0 + scale) + shift).astype(BF16)
        g = _dot(h, wg_ref[...])
        u = _dot(h, wu_ref[...])
        a = (g * jax.nn.sigmoid(g) * u).astype(BF16)
        y = _dot(a, wo_ref[...])
        r = alpha * x + (0.5 * (1.0 + gate)) * y
        o_ref[r0:r0 + rows, :] = _layer_norm(r, lg_ref[sub:sub + 1, :], lb_ref[sub:sub + 1, :])


def _ffn(x, mod_all, w_in, w_out, ln_g, ln_b, *, layer, which, seq, alpha, tm=1024, rows=256):
    t, d = x.shape
    f = w_out.shape[2]
    k0 = 0 if which == 0 else 6
    sub = 0 if which == 0 else 2
    per_seq = seq // tm
    kern = functools.partial(_ffn_kernel, k0=k0, sub=sub, alpha=alpha, rows=rows)
    return pl.pallas_call(
        kern,
        grid=(t // tm,),
        in_specs=[
            pl.BlockSpec((tm, d), lambda i: (i, 0)),
            pl.BlockSpec((None, 1, 3 * N_SUB, d), lambda i: (layer, i // per_seq, 0, 0)),
            pl.BlockSpec((None, None, d, f), lambda i: (layer, which, 0, 0)),
            pl.BlockSpec((None, None, d, f), lambda i: (layer, which, 0, 1)),
            pl.BlockSpec((None, None, f, d), lambda i: (layer, which, 0, 0)),
            pl.BlockSpec((None, N_SUB, d), lambda i: (layer, 0, 0)),
            pl.BlockSpec((None, N_SUB, d), lambda i: (layer, 0, 0)),
        ],
        out_specs=pl.BlockSpec((tm, d), lambda i: (i, 0)),
        out_shape=jax.ShapeDtypeStruct((t, d), F32),
        compiler_params=_params(1),
        name=f"ffn_l{layer}_{which}",
    )(x, mod_all, w_in, w_in, w_out, ln_g, ln_b)


def _rotary_tables(seq, head_dim):
    rot = head_dim // ROT_FRACTION
    half = rot // 2
    inv_freq = jnp.power(ROPE_THETA, -jnp.arange(half, dtype=F32) * 2.0 / rot)
    ang = jnp.arange(seq, dtype=F32)[:, None] * inv_freq[None, :]
    cos, sin = jnp.cos(ang), jnp.sin(ang)
    ones = jnp.ones((seq, head_dim - rot), F32)
    zeros = jnp.zeros((seq, head_dim - rot), F32)
    zh = jnp.zeros((seq, half), F32)
    c_tab = jnp.concatenate([cos, cos, ones], axis=1)
    up_tab = jnp.concatenate([zh, sin, zeros], axis=1)
    dn_tab = jnp.concatenate([-sin, zh, zeros], axis=1)
    return c_tab, up_tab, dn_tab


def _qkv_kernel(x_ref, mod_ref, w_ref, c_ref, up_ref, dn_ref, q_ref, k_ref, v_ref, kbar_ref, *, half):
    x = x_ref[...]
    d = x.shape[1]
    hd = c_ref.shape[1]
    shift = mod_ref[0, 3:4, :]
    scale = mod_ref[0, 4:5, :]
    h = (x * (1.0 + scale) + shift).astype(BF16)
    qkv = _dot(h, w_ref[...])
    c_tab, up_tab, dn_tab = c_ref[...], up_ref[...], dn_ref[...]

    def rotate(t):
        parts = []
        for hh in range(d // hd):
            th = t[:, hh * hd:(hh + 1) * hd]
            parts.append(th * c_tab
                         + pltpu.roll(th, half, 1) * up_tab
                         + pltpu.roll(th, hd - half, 1) * dn_tab)
        return jnp.concatenate(parts, axis=1)

    q = rotate(qkv[:, :d])
    k = rotate(qkv[:, d:2 * d])
    q_ref[...] = q.astype(q_ref.dtype)
    k_ref[...] = k.astype(k_ref.dtype)
    v_ref[...] = qkv[:, 2 * d:].astype(v_ref.dtype)
    for r in range(k.shape[0] // MOBA_BLOCK):
        kbar_ref[0, r:r + 1, :] = jnp.mean(k[r * MOBA_BLOCK:(r + 1) * MOBA_BLOCK], axis=0, keepdims=True)


def _qkv(x, mod_all, w, tabs, *, layer, seq, tm=512):
    t, d = x.shape
    hd = d // N_HEADS
    per_seq = seq // tm
    nb = tm // MOBA_BLOCK
    kern = functools.partial(_qkv_kernel, half=hd // ROT_FRACTION // 2)
    tab_spec = pl.BlockSpec((tm, hd), lambda i: (i % per_seq, 0))
    tok_spec = pl.BlockSpec((tm, d), lambda i: (i, 0))
    return pl.pallas_call(
        kern,
        grid=(t // tm,),
        in_specs=[
            tok_spec,
            pl.BlockSpec((None, 1, 3 * N_SUB, d), lambda i: (layer, i // per_seq, 0, 0)),
            pl.BlockSpec((None, d, 3 * d), lambda i: (layer, 0, 0)),
            tab_spec, tab_spec, tab_spec,
        ],
        out_specs=[tok_spec, tok_spec, tok_spec,
                   pl.BlockSpec((1, nb, d), lambda i: (i, 0, 0))],
        out_shape=[jax.ShapeDtypeStruct((t, d), BF16)] * 3
        + [jax.ShapeDtypeStruct((t // tm, nb, d), F32)],
        compiler_params=_params(1),
        name=f"qkv_l{layer}",
    )(x, mod_all, w, *tabs)


def _proj_kernel(x_ref, mod_ref, w_ref, o_ref, *, r_shift, r_scale):
    x = x_ref[...]
    shift = mod_ref[0, r_shift:r_shift + 1, :]
    scale = mod_ref[0, r_scale:r_scale + 1, :]
    h = (x * (1.0 + scale) + shift).astype(BF16)
    o_ref[...] = _dot(h, w_ref[...]).astype(o_ref.dtype)


def _proj(x, mod, w, *, mod_layer, w_layer, r_shift, r_scale, seq, tm=512, name="proj"):
    t, d = x.shape
    n = w.shape[2]
    rows = mod.shape[2]
    per_seq = seq // tm
    kern = functools.partial(_proj_kernel, r_shift=r_shift, r_scale=r_scale)
    return pl.pallas_call(
        kern,
        grid=(t // tm,),
        in_specs=[
            pl.BlockSpec((tm, d), lambda i: (i, 0)),
            pl.BlockSpec((None, 1, rows, d), lambda i: (mod_layer, i // per_seq, 0, 0)),
            pl.BlockSpec((None, d, n), lambda i: (w_layer, 0, 0)),
        ],
        out_specs=pl.BlockSpec((tm, n), lambda i: (i, 0)),
        out_shape=jax.ShapeDtypeStruct((t, n), BF16),
        compiler_params=_params(1),
        name=name,
    )(x, mod, w)


def _oproj_kernel(a_ref, x_ref, mod_ref, w_ref, lg_ref, lb_ref, o_ref, *, alpha):
    x = x_ref[...]
    gate = mod_ref[0, 5:6, :]
    y = _dot(a_ref[...], w_ref[...])
    r = alpha * x + (1.0 + gate) * y
    o_ref[...] = _layer_norm(r, lg_ref[1:2, :], lb_ref[1:2, :])


def _oproj(att, x, mod_all, w, ln_g, ln_b, *, layer, seq, alpha, tm=512):
    t, d = x.shape
    per_seq = seq // tm
    kern = functools.partial(_oproj_kernel, alpha=alpha)
    tok_spec = pl.BlockSpec((tm, d), lambda i: (i, 0))
    return pl.pallas_call(
        kern,
        grid=(t // tm,),
        in_specs=[
            tok_spec, tok_spec,
            pl.BlockSpec((None, 1, 3 * N_SUB, d), lambda i: (layer, i // per_seq, 0, 0)),
            pl.BlockSpec((None, d, d), lambda i: (layer, 0, 0)),
            pl.BlockSpec((None, N_SUB, d), lambda i: (layer, 0, 0)),
            pl.BlockSpec((None, N_SUB, d), lambda i: (layer, 0, 0)),
        ],
        out_specs=tok_spec,
        out_shape=jax.ShapeDtypeStruct((t, d), F32),
        compiler_params=_params(1),
        name=f"oproj_l{layer}",
    )(att, x, mod_all, w, ln_g, ln_b)


def _moba_kernel(q_ref, k_ref, v_ref, kbar_ref, o_ref, vt_ref, s_ref, p_ref, *, nblk, scale):
    blk = MOBA_BLOCK
    hd = q_ref.shape[1]
    grows = -(-nblk // BF16_SUBLANES) * BF16_SUBLANES
    kb = jnp.concatenate([kbar_ref[0], jnp.zeros((grows - nblk, hd), F32)], axis=0)
    kb_hi = kb.astype(BF16)
    kb_lo = (kb - kb_hi.astype(F32)).astype(BF16)
    key = lax.broadcasted_iota(jnp.int32, (blk, blk), 0)
    qry = lax.broadcasted_iota(jnp.int32, (blk, blk), 1)
    causal = key <= qry
    grow = lax.broadcasted_iota(jnp.int32, (grows, blk), 0).astype(F32)
    c2 = scale * LOG2E

    for j in range(nblk):
        vt_ref[:, j * blk:(j + 1) * blk] = v_ref[j * blk:(j + 1) * blk, :].astype(F32).T.astype(BF16)

    def scores(i):
        qi = q_ref[i * blk:(i + 1) * blk, :]
        cut = ((i + 1) // 2) * blk
        for lo, hi in ((0, cut), (cut, (i + 1) * blk)):
            if hi > lo:
                s_ref[i % 2, lo:hi, :] = _dot_t(k_ref[lo:hi, :], qi) * c2

    def weighted_values(i, l):
        acc = _dot(vt_ref[:, 0:(i + 1) * blk], p_ref[i % 2, 0:(i + 1) * blk, :])
        o_ref[i * blk:(i + 1) * blk, :] = (acc * (1.0 / l)).T.astype(o_ref.dtype)

    scores(0)
    l_prev = None
    for i in range(nblk):
        nk = (i + 1) * blk
        qi = q_ref[i * blk:nk, :]
        if i + 1 < nblk:
            scores(i + 1)
        if i > 0:
            weighted_values(i - 1, l_prev)

        if i > MOBA_TOPK:
            gate = _dot_t(kb_hi, qi) + _dot_t(kb_lo, qi)
            gate = jnp.where(grow < float(i), gate, -jnp.inf)
            sel = jnp.full((grows, blk), NEG_INF, F32)
            for _ in range(MOBA_TOPK):
                top = jnp.max(gate, axis=0, keepdims=True)
                first = jnp.min(jnp.where(gate == top, grow, float(grows)), axis=0, keepdims=True)
                hit = grow == first
                sel = jnp.where(hit, 0.0, sel)
                gate = jnp.where(hit, -jnp.inf, gate)

        s = jnp.where(causal, s_ref[i % 2, i * blk:nk, :], NEG_INF)
        s_ref[i % 2, i * blk:nk, :] = s
        m = jnp.max(s, axis=0, keepdims=True)
        for j in range(i):
            if i > MOBA_TOPK:
                s = s_ref[i % 2, j * blk:(j + 1) * blk, :] + sel[j:j + 1, :]
                s_ref[i % 2, j * blk:(j + 1) * blk, :] = s
            else:
                s = s_ref[i % 2, j * blk:(j + 1) * blk, :]
            m = jnp.maximum(m, jnp.max(s, axis=0, keepdims=True))
        l = jnp.zeros((1, blk), F32)
        for j in range(i + 1):
            p = jnp.exp2(s_ref[i % 2, j * blk:(j + 1) * blk, :] - m)
            l = l + jnp.sum(p, axis=0, keepdims=True)
            p_ref[i % 2, j * blk:(j + 1) * blk, :] = p.astype(BF16)
        l_prev = l
    weighted_values(nblk - 1, l_prev)


def _moba(q, k, v, kbar, *, batch, seq):
    t, d = q.shape
    hd = d // N_HEADS
    nblk = seq // MOBA_BLOCK
    kern = functools.partial(_moba_kernel, nblk=nblk, scale=hd ** -0.5)
    head_spec = pl.BlockSpec((seq, hd), lambda b, h: (b, h))
    return pl.pallas_call(
        kern,
        grid=(batch, N_HEADS),
        in_specs=[head_spec, head_spec, head_spec,
                  pl.BlockSpec((1, nblk, hd), lambda b, h: (b, 0, h))],
        out_specs=head_spec,
        out_shape=jax.ShapeDtypeStruct((t, d), BF16),
        scratch_shapes=[pltpu.VMEM((hd, seq), BF16),
                        pltpu.VMEM((2, seq, MOBA_BLOCK), F32),
                        pltpu.VMEM((2, seq, MOBA_BLOCK), BF16)],
        compiler_params=_params(2),
        name="moba",
    )(q, k, v, kbar)


def _softplus2(z2):
    neg_abs = lax.bitcast_convert_type(
        lax.bitcast_convert_type(z2, jnp.uint32) | jnp.uint32(0x80000000), F32)
    return jnp.maximum(z2, 0.0) + jnp.log2(1.0 + jnp.exp2(neg_abs))


def _sb_kernel(q_ref, k_ref, v_ref, o_ref, vt_ref, z_ref, w_ref, *, nblk, scale):
    blk = SB_BLOCK
    key = lax.broadcasted_iota(jnp.int32, (blk, blk), 0)
    qry = lax.broadcasted_iota(jnp.int32, (blk, blk), 1)
    strict = key < qry
    later = (qry > key).astype(BF16)
    later2 = jnp.concatenate([later, later], axis=1)
    c2 = scale * LOG2E

    def sum_later(sp):
        hi = sp.astype(BF16)
        lo = (sp - hi.astype(F32)).astype(BF16)
        return _dot(later2, jnp.concatenate([hi, lo], axis=0))

    for j in range(nblk):
        vt_ref[:, j * blk:(j + 1) * blk] = v_ref[j * blk:(j + 1) * blk, :].astype(F32).T.astype(BF16)

    def logits(i):
        z_ref[i % 2, 0:(i + 1) * blk, :] = _dot_t(k_ref[0:(i + 1) * blk, :], q_ref[i * blk:(i + 1) * blk, :]) * c2

    def weighted_values(i):
        acc = _dot(vt_ref[:, 0:(i + 1) * blk], w_ref[i % 2, 0:(i + 1) * blk, :])
        o_ref[i * blk:(i + 1) * blk, :] = acc.T.astype(o_ref.dtype)

    logits(0)
    for i in range(nblk):
        if i + 1 < nblk:
            logits(i + 1)
        if i > 0:
            weighted_values(i - 1)
        tail = None
        for j in range(i, -1, -1):
            z = z_ref[i % 2, j * blk:(j + 1) * blk, :]
            sp = _softplus2(z)
            if j == i:
                sp = jnp.where(strict, sp, 0.0)
            e = z - sp - sum_later(sp)
            w = jnp.exp2(e if tail is None else e - tail)
            if j == i:
                w = jnp.where(strict, w, 0.0)
            w_ref[i % 2, j * blk:(j + 1) * blk, :] = w.astype(BF16)
            if j > 0:
                part = jnp.sum(sp, axis=0, keepdims=True)
                tail = part if tail is None else tail + part
    weighted_values(nblk - 1)


def _sb(q, kv, *, batch, seq):
    t, d = q.shape
    hd = d // N_HEADS
    nblk = seq // SB_BLOCK
    kern = functools.partial(_sb_kernel, nblk=nblk, scale=hd ** -0.5)
    return pl.pallas_call(
        kern,
        grid=(batch, N_HEADS),
        in_specs=[pl.BlockSpec((seq, hd), lambda b, h: (b, h)),
                  pl.BlockSpec((seq, hd), lambda b, h: (b, h)),
                  pl.BlockSpec((seq, hd), lambda b, h: (b, N_HEADS + h))],
        out_specs=pl.BlockSpec((seq, hd), lambda b, h: (b, h)),
        out_shape=jax.ShapeDtypeStruct((t, d), BF16),
        scratch_shapes=[pltpu.VMEM((hd, seq), BF16),
                        pltpu.VMEM((2, seq, SB_BLOCK), F32),
                        pltpu.VMEM((2, seq, SB_BLOCK), BF16)],
        compiler_params=_params(2),
        name="stickbreak",
    )(q, kv, kv)


def kernel(x, c, w_ada, b_ada, ln_g, ln_b, w_ffn_in, w_ffn_out, w_qkv_a, w_q_b, w_kv_ada, b_kv_ada, w_kv_b, w_o):
    bsz, seq, d = x.shape
    depth = w_ada.shape[0]
    n_a = w_qkv_a.shape[0]
    alpha = (2.0 * depth) ** 0.25
    t = bsz * seq
    assert seq % MOBA_BLOCK == 0 and seq % SB_BLOCK == 0 and d % (N_HEADS * LANES) == 0

    mod_all = _ada(c, w_ada, b_ada).reshape(depth, bsz, 3 * N_SUB, d)
    kv_mod = _ada(c, w_kv_ada[None], b_kv_ada[None]).reshape(1, bsz, 2, d)

    w_in = w_ffn_in.astype(BF16)
    w_out = w_ffn_out.astype(BF16)
    w_qkv = w_qkv_a.astype(BF16)
    w_q = w_q_b.astype(BF16)
    w_kv = w_kv_b.astype(BF16)[None]
    w_ob = w_o.astype(BF16)
    tabs = _rotary_tables(seq, d // N_HEADS)

    xf = x.reshape(t, d)
    kv = None
    for l in range(depth):
        if l == n_a:
            kv = _proj(xf, kv_mod, w_kv, mod_layer=0, w_layer=0, r_shift=0, r_scale=1, seq=seq, name="kv_proj")
        xf = _ffn(xf, mod_all, w_in, w_out, ln_g, ln_b, layer=l, which=0, seq=seq, alpha=alpha)
        if l < n_a:
            q, k, v, kbar = _qkv(xf, mod_all, w_qkv, tabs, layer=l, seq=seq)
            att = _moba(q, k, v, kbar.reshape(bsz, seq // MOBA_BLOCK, d), batch=bsz, seq=seq)
        else:
            q = _proj(xf, mod_all, w_q, mod_layer=l, w_layer=l - n_a, r_shift=3, r_scale=4, seq=seq,
                      name=f"q_proj_l{l}")
            att = _sb(q, kv, batch=bsz, seq=seq)
        xf = _oproj(att, xf, mod_all, w_ob, ln_g, ln_b, layer=l, seq=seq, alpha=alpha)
        xf = _ffn(xf, mod_all, w_in, w_out, ln_g, ln_b, layer=l, which=1, seq=seq, alpha=alpha)
    return xf.reshape(bsz, seq, d)
```

```python
import functools

import jax
import jax.numpy as jnp
from jax import lax
from jax.experimental import pallas as pl
from jax.experimental.pallas import tpu as pltpu

N_HEADS = 8
ROT_FRACTION = 4
ROPE_THETA = 500000.0
MOBA_BLOCK = 256
MOBA_TOPK = 3
SB_BLOCK = 256
SB_NEAR = 2
SB_DEAD_LOG2 = 160.0
LN_EPS = 1e-5
NEG_INF = -1e30
N_SUB = 3
LANES = 128
BF16_SUBLANES = 16
LOG2E = 1.4426950408889634

F32 = jnp.float32
BF16 = jnp.bfloat16

VMEM_LIMIT = 56 * 1024 * 1024
FFN_TILE = 1024


def _params(n_axes):
    return pltpu.CompilerParams(
        dimension_semantics=("parallel",) * n_axes, vmem_limit_bytes=VMEM_LIMIT)


def _dot(a, b):
    return jnp.dot(a, b, preferred_element_type=F32)


def _dot_t(a, b):
    return lax.dot_general(a, b, (((1,), (1,)), ((), ())), preferred_element_type=F32)


def _dot_tn(a, b):
    return lax.dot_general(a, b, (((0,), (0,)), ((), ())), preferred_element_type=F32)


def _layer_norm(r, g, b):
    mu = jnp.mean(r, axis=-1, keepdims=True)
    d = r - mu
    var = jnp.mean(d * d, axis=-1, keepdims=True)
    return d * lax.rsqrt(var + LN_EPS) * g + b


def _ada_kernel(c_ref, w_ref, b_ref, o_ref):
    c = c_ref[...]
    cond = (c * jax.nn.sigmoid(c)).astype(BF16)
    o_ref[...] = _dot(cond, w_ref[...].astype(BF16)) + b_ref[...]


def _ada(c, w, b, tn=1024):
    nl, d, n = w.shape
    bsz = c.shape[0]
    return pl.pallas_call(
        _ada_kernel,
        grid=(nl, n // tn),
        in_specs=[
            pl.BlockSpec((bsz, d), lambda l, j: (0, 0)),
            pl.BlockSpec((None, d, tn), lambda l, j: (l, 0, j)),
            pl.BlockSpec((None, 1, tn), lambda l, j: (l, 0, j)),
        ],
        out_specs=pl.BlockSpec((None, bsz, tn), lambda l, j: (l, 0, j)),
        out_shape=jax.ShapeDtypeStruct((nl, bsz, n), F32),
        compiler_params=_params(2),
        name="ada",
    )(c, w, b.reshape(nl, 1, n))


def _ffn_kernel(*refs, k0, sub, alpha, rows, mixer_out, proj_rows):
    it = iter(refs)
    x_ref, mod_ref = next(it), next(it)
    if mixer_out:
        att_ref, wmix_ref = next(it), next(it)
    wg_ref, wu_ref, wo_ref, lg_ref, lb_ref = (next(it) for _ in range(5))
    if proj_rows:
        pmod_ref, wproj_ref = next(it), next(it)
    o_ref = next(it)
    if proj_rows:
        p_ref = next(it)

    shift = mod_ref[0, k0:k0 + 1, :]
    scale = mod_ref[0, k0 + 1:k0 + 2, :]
    gate = mod_ref[0, k0 + 2:k0 + 3, :]
    for r0 in range(0, x_ref.shape[0], rows):
        x = x_ref[r0:r0 + rows, :]
        if mixer_out:
            y = _dot(att_ref[r0:r0 + rows, :], wmix_ref[...])
            x = _layer_norm(alpha * x + (1.0 + mod_ref[0, 5:6, :]) * y, lg_ref[1:2, :], lb_ref[1:2, :])
        h = (x * (1.0 + scale) + shift).astype(BF16)
        g = _dot(h, wg_ref[...])
        u = _dot(h, wu_ref[...])
        a = (g * jax.nn.sigmoid(g) * u).astype(BF16)
        y = _dot(a, wo_ref[...])
        r = alpha * x + (0.5 * (1.0 + gate)) * y
        x = _layer_norm(r, lg_ref[sub:sub + 1, :], lb_ref[sub:sub + 1, :])
        o_ref[r0:r0 + rows, :] = x
        if proj_rows:
            r_shift, r_scale = proj_rows
            hp = (x * (1.0 + pmod_ref[0, r_scale:r_scale + 1, :]) + pmod_ref[0, r_shift:r_shift + 1, :]).astype(BF16)
            p_ref[r0:r0 + rows, :] = _dot(hp, wproj_ref[...]).astype(p_ref.dtype)


def _ffn_tile(d, f, seq, rows, *, with_mixer, n_proj):
    weights = 2 * (3 * d * f + (d * d if with_mixer else 0) + d * n_proj)
    per_token = 2 * (4 * d + 4 * d + (2 * d if with_mixer else 0) + 2 * n_proj)
    temporaries = rows * (2 * f * 4 + f * 2 + 4 * d * 4)
    tm = FFN_TILE
    while tm > rows and (seq % tm or weights + per_token * tm + temporaries > VMEM_LIMIT):
        tm //= 2
    return tm


def _ffn(x, mod_all, w_in, w_out, ln_g, ln_b, *, layer, which, seq, alpha, mixer=None, proj=None, rows=256):
    t, d = x.shape
    f = w_out.shape[2]
    k0 = 0 if which == 0 else 6
    sub = 0 if which == 0 else 2
    n_proj = 0 if proj is None else proj[3].shape[2]
    tm = _ffn_tile(d, f, seq, rows, with_mixer=mixer is not None, n_proj=n_proj)
    per_seq = seq // tm
    tok_spec = pl.BlockSpec((tm, d), lambda i: (i, 0))
    ln_spec = pl.BlockSpec((None, N_SUB, d), lambda i: (layer, 0, 0))
    args = [x, mod_all]
    in_specs = [tok_spec, pl.BlockSpec((None, 1, 3 * N_SUB, d), lambda i: (layer, i // per_seq, 0, 0))]
    if mixer is not None:
        att, w_mix = mixer
        args += [att, w_mix]
        in_specs += [tok_spec, pl.BlockSpec((None, d, d), lambda i: (layer, 0, 0))]
    args += [w_in, w_in, w_out, ln_g, ln_b]
    in_specs += [
        pl.BlockSpec((None, None, d, f), lambda i: (layer, which, 0, 0)),
        pl.BlockSpec((None, None, d, f), lambda i: (layer, which, 0, 1)),
        pl.BlockSpec((None, None, f, d), lambda i: (layer, which, 0, 0)),
        ln_spec, ln_spec,
    ]
    out_specs = [tok_spec]
    out_shape = [jax.ShapeDtypeStruct((t, d), F32)]
    proj_rows = None
    if proj is not None:
        pmod, mod_layer, proj_rows, w_proj, w_layer = proj
        n = w_proj.shape[2]
        args += [pmod, w_proj]
        in_specs += [pl.BlockSpec((None, 1, pmod.shape[2], d), lambda i: (mod_layer, i // per_seq, 0, 0)),
                     pl.BlockSpec((None, d, n), lambda i: (w_layer, 0, 0))]
        out_specs.append(pl.BlockSpec((tm, n), lambda i: (i, 0)))
        out_shape.append(jax.ShapeDtypeStruct((t, n), BF16))
    kern = functools.partial(_ffn_kernel, k0=k0, sub=sub, alpha=alpha, rows=rows,
                             mixer_out=mixer is not None, proj_rows=proj_rows)
    outs = pl.pallas_call(
        kern,
        grid=(t // tm,),
        in_specs=in_specs,
        out_specs=out_specs,
        out_shape=out_shape,
        compiler_params=_params(1),
        name=f"ffn_l{layer}_{which}",
    )(*args)
    return outs if proj is not None else outs[0]


def _rotary_tables(seq, head_dim):
    rot = head_dim // ROT_FRACTION
    half = rot // 2
    inv_freq = jnp.power(ROPE_THETA, -jnp.arange(half, dtype=F32) * 2.0 / rot)
    ang = jnp.arange(seq, dtype=F32)[:, None] * inv_freq[None, :]
    cos, sin = jnp.cos(ang), jnp.sin(ang)
    ones = jnp.ones((seq, head_dim - rot), F32)
    zeros = jnp.zeros((seq, head_dim - rot), F32)
    zh = jnp.zeros((seq, half), F32)
    c_tab = jnp.concatenate([cos, cos, ones], axis=1)
    up_tab = jnp.concatenate([zh, sin, zeros], axis=1)
    dn_tab = jnp.concatenate([-sin, zh, zeros], axis=1)
    return c_tab, up_tab, dn_tab


def _qkv_kernel(x_ref, mod_ref, w_ref, c_ref, up_ref, dn_ref, q_ref, k_ref, v_ref, kbar_ref, *, half):
    x = x_ref[...]
    d = x.shape[1]
    hd = c_ref.shape[1]
    shift = mod_ref[0, 3:4, :]
    scale = mod_ref[0, 4:5, :]
    h = (x * (1.0 + scale) + shift).astype(BF16)
    qkv = _dot(h, w_ref[...])
    c_tab, up_tab, dn_tab = c_ref[...], up_ref[...], dn_ref[...]

    def rotate(t):
        parts = []
        for hh in range(d // hd):
            th = t[:, hh * hd:(hh + 1) * hd]
            parts.append(th * c_tab
                         + pltpu.roll(th, half, 1) * up_tab
                         + pltpu.roll(th, hd - half, 1) * dn_tab)
        return jnp.concatenate(parts, axis=1)

    q = rotate(qkv[:, :d])
    k = rotate(qkv[:, d:2 * d])
    q_ref[...] = q.astype(q_ref.dtype)
    k_ref[...] = k.astype(k_ref.dtype)
    v_ref[...] = qkv[:, 2 * d:].astype(v_ref.dtype)
    for r in range(k.shape[0] // MOBA_BLOCK):
        kbar_ref[0, r:r + 1, :] = jnp.mean(k[r * MOBA_BLOCK:(r + 1) * MOBA_BLOCK], axis=0, keepdims=True)


def _qkv(x, mod_all, w, tabs, *, layer, seq, tm=512):
    t, d = x.shape
    hd = d // N_HEADS
    per_seq = seq // tm
    nb = tm // MOBA_BLOCK
    kern = functools.partial(_qkv_kernel, half=hd // ROT_FRACTION // 2)
    tab_spec = pl.BlockSpec((tm, hd), lambda i: (i % per_seq, 0))
    tok_spec = pl.BlockSpec((tm, d), lambda i: (i, 0))
    return pl.pallas_call(
        kern,
        grid=(t // tm,),
        in_specs=[
            tok_spec,
            pl.BlockSpec((None, 1, 3 * N_SUB, d), lambda i: (layer, i // per_seq, 0, 0)),
            pl.BlockSpec((None, d, 3 * d), lambda i: (layer, 0, 0)),
            tab_spec, tab_spec, tab_spec,
        ],
        out_specs=[tok_spec, tok_spec, tok_spec,
                   pl.BlockSpec((1, nb, d), lambda i: (i, 0, 0))],
        out_shape=[jax.ShapeDtypeStruct((t, d), BF16)] * 3
        + [jax.ShapeDtypeStruct((t // tm, nb, d), F32)],
        compiler_params=_params(1),
        name=f"qkv_l{layer}",
    )(x, mod_all, w, *tabs)


def _proj_kernel(x_ref, mod_ref, w_ref, o_ref, *, r_shift, r_scale):
    x = x_ref[...]
    shift = mod_ref[0, r_shift:r_shift + 1, :]
    scale = mod_ref[0, r_scale:r_scale + 1, :]
    h = (x * (1.0 + scale) + shift).astype(BF16)
    o_ref[...] = _dot(h, w_ref[...]).astype(o_ref.dtype)


def _proj(x, mod, w, *, mod_layer, w_layer, r_shift, r_scale, seq, tm=512, name="proj"):
    t, d = x.shape
    n = w.shape[2]
    rows = mod.shape[2]
    per_seq = seq // tm
    kern = functools.partial(_proj_kernel, r_shift=r_shift, r_scale=r_scale)
    return pl.pallas_call(
        kern,
        grid=(t // tm,),
        in_specs=[
            pl.BlockSpec((tm, d), lambda i: (i, 0)),
            pl.BlockSpec((None, 1, rows, d), lambda i: (mod_layer, i // per_seq, 0, 0)),
            pl.BlockSpec((None, d, n), lambda i: (w_layer, 0, 0)),
        ],
        out_specs=pl.BlockSpec((tm, n), lambda i: (i, 0)),
        out_shape=jax.ShapeDtypeStruct((t, n), BF16),
        compiler_params=_params(1),
        name=name,
    )(x, mod, w)


def _moba_kernel(q_ref, k_ref, v_ref, kbar_ref, o_ref, vt_ref, s_ref, p_ref, *, nblk, scale):
    blk = MOBA_BLOCK
    hd = q_ref.shape[1]
    grows = -(-nblk // BF16_SUBLANES) * BF16_SUBLANES
    kb = jnp.concatenate([kbar_ref[0], jnp.zeros((grows - nblk, hd), F32)], axis=0)
    kb_hi = kb.astype(BF16)
    kb_lo = (kb - kb_hi.astype(F32)).astype(BF16)
    key = lax.broadcasted_iota(jnp.int32, (blk, blk), 0)
    qry = lax.broadcasted_iota(jnp.int32, (blk, blk), 1)
    causal = key <= qry
    grow = lax.broadcasted_iota(jnp.int32, (grows, blk), 0).astype(F32)
    c2 = scale * LOG2E

    for j in range(nblk):
        vt_ref[:, j * blk:(j + 1) * blk] = v_ref[j * blk:(j + 1) * blk, :].astype(F32).T.astype(BF16)

    def scores(i):
        qi = q_ref[i * blk:(i + 1) * blk, :]
        cut = ((i + 1) // 2) * blk
        for lo, hi in ((0, cut), (cut, (i + 1) * blk)):
            if hi > lo:
                s_ref[i % 2, lo:hi, :] = _dot_t(k_ref[lo:hi, :], qi) * c2

    def weighted_values(i, l):
        acc = _dot(vt_ref[:, 0:(i + 1) * blk], p_ref[i % 2, 0:(i + 1) * blk, :])
        o_ref[i * blk:(i + 1) * blk, :] = (acc * (1.0 / l)).T.astype(o_ref.dtype)

    scores(0)
    l_prev = None
    for i in range(nblk):
        nk = (i + 1) * blk
        qi = q_ref[i * blk:nk, :]
        if i + 1 < nblk:
            scores(i + 1)
        if i > 0:
            weighted_values(i - 1, l_prev)

        if i > MOBA_TOPK:
            gate = _dot_t(kb_hi, qi) + _dot_t(kb_lo, qi)
            gate = jnp.where(grow < float(i), gate, -jnp.inf)
            sel = jnp.full((grows, blk), NEG_INF, F32)
            for _ in range(MOBA_TOPK):
                top = jnp.max(gate, axis=0, keepdims=True)
                first = jnp.min(jnp.where(gate == top, grow, float(grows)), axis=0, keepdims=True)
                hit = grow == first
                sel = jnp.where(hit, 0.0, sel)
                gate = jnp.where(hit, -jnp.inf, gate)

        s = jnp.where(causal, s_ref[i % 2, i * blk:nk, :], NEG_INF)
        s_ref[i % 2, i * blk:nk, :] = s
        m = jnp.max(s, axis=0, keepdims=True)
        for j in range(i):
            if i > MOBA_TOPK:
                s = s_ref[i % 2, j * blk:(j + 1) * blk, :] + sel[j:j + 1, :]
                s_ref[i % 2, j * blk:(j + 1) * blk, :] = s
            else:
                s = s_ref[i % 2, j * blk:(j + 1) * blk, :]
            m = jnp.maximum(m, jnp.max(s, axis=0, keepdims=True))
        l = jnp.zeros((1, blk), F32)
        for j in range(i + 1):
            p = jnp.exp2(s_ref[i % 2, j * blk:(j + 1) * blk, :] - m)
            l = l + jnp.sum(p, axis=0, keepdims=True)
            p_ref[i % 2, j * blk:(j + 1) * blk, :] = p.astype(BF16)
        l_prev = l
    weighted_values(nblk - 1, l_prev)


def _moba(q, k, v, kbar, *, batch, seq):
    t, d = q.shape
    hd = d // N_HEADS
    nblk = seq // MOBA_BLOCK
    kern = functools.partial(_moba_kernel, nblk=nblk, scale=hd ** -0.5)
    head_spec = pl.BlockSpec((seq, hd), lambda b, h: (b, h))
    return pl.pallas_call(
        kern,
        grid=(batch, N_HEADS),
        in_specs=[head_spec, head_spec, head_spec,
                  pl.BlockSpec((1, nblk, hd), lambda b, h: (b, 0, h))],
        out_specs=head_spec,
        out_shape=jax.ShapeDtypeStruct((t, d), BF16),
        scratch_shapes=[pltpu.VMEM((hd, seq), BF16),
                        pltpu.VMEM((2, seq, MOBA_BLOCK), F32),
                        pltpu.VMEM((2, seq, MOBA_BLOCK), BF16)],
        compiler_params=_params(2),
        name="moba",
    )(q, k, v, kbar)


def _softplus2(z2):
    neg_abs = lax.bitcast_convert_type(
        lax.bitcast_convert_type(z2, jnp.uint32) | jnp.uint32(0x80000000), F32)
    return jnp.maximum(z2, 0.0) + jnp.log2(1.0 + jnp.exp2(neg_abs))


def _sb_kernel(q_ref, k_ref, v_ref, o_ref, vt_ref, z_ref, w_ref, acc_ref, tail_ref, *, nblk, scale):
    blk = SB_BLOCK
    key = lax.broadcasted_iota(jnp.int32, (blk, blk), 0)
    qry = lax.broadcasted_iota(jnp.int32, (blk, blk), 1)
    strict = key < qry
    later = (qry > key).astype(BF16)
    later2 = jnp.concatenate([later, later], axis=1)
    c2 = scale * LOG2E

    def sum_later(sp):
        hi = sp.astype(BF16)
        lo = (sp - hi.astype(F32)).astype(BF16)
        return _dot(later2, jnp.concatenate([hi, lo], axis=0))

    for j in range(nblk):
        vt_ref[:, j * blk:(j + 1) * blk] = v_ref[j * blk:(j + 1) * blk, :].astype(F32).T.astype(BF16)

    def logits(i, b0, b1):
        z_ref[i % 2, b0 * blk:b1 * blk, :] = _dot_t(k_ref[b0 * blk:b1 * blk, :], q_ref[i * blk:(i + 1) * blk, :]) * c2

    def block_weights(i, j, tail):
        z = z_ref[i % 2, j * blk:(j + 1) * blk, :]
        sp = _softplus2(z)
        if j == i:
            sp = jnp.where(strict, sp, 0.0)
        e = z - sp - sum_later(sp)
        w = jnp.exp2(e if tail is None else e - tail)
        if j == i:
            w = jnp.where(strict, w, 0.0)
        w_ref[i % 2, j * blk:(j + 1) * blk, :] = w.astype(BF16)
        if j == 0:
            return tail
        part = jnp.sum(sp, axis=0, keepdims=True)
        return part if tail is None else tail + part

    def near_values(i):
        b0 = max(i + 1 - SB_NEAR, 0)
        acc = _dot(vt_ref[:, b0 * blk:(i + 1) * blk], w_ref[i % 2, b0 * blk:(i + 1) * blk, :])
        if b0 > 0:
            acc_ref[i] = acc
        o_ref[i * blk:(i + 1) * blk, :] = acc.T.astype(o_ref.dtype)

    logits(0, 0, 1)
    least_tail = None
    for i in range(nblk):
        if i + 1 < nblk:
            logits(i + 1, max(i + 2 - SB_NEAR, 0), i + 2)
        if i > 0:
            near_values(i - 1)
        b0 = max(i + 1 - SB_NEAR, 0)
        tail = None
        for j in range(i, b0 - 1, -1):
            tail = block_weights(i, j, tail)
        if b0 > 0:
            tail_ref[i:i + 1, :] = tail
            least_tail = tail if least_tail is None else jnp.minimum(least_tail, tail)
    near_values(nblk - 1)

    if nblk > SB_NEAR:
        @pl.when(jnp.min(least_tail) < SB_DEAD_LOG2)
        def _():
            for i in range(SB_NEAR, nblk):
                b1 = i + 1 - SB_NEAR
                logits(i, 0, b1)
                tail = tail_ref[i:i + 1, :]
                for j in range(b1 - 1, -1, -1):
                    tail = block_weights(i, j, tail)
                acc = acc_ref[i] + _dot(vt_ref[:, 0:b1 * blk], w_ref[i % 2, 0:b1 * blk, :])
                o_ref[i * blk:(i + 1) * blk, :] = acc.T.astype(o_ref.dtype)


def _sb(q, kv, *, batch, seq):
    t, d = q.shape
    hd = d // N_HEADS
    nblk = seq // SB_BLOCK
    kern = functools.partial(_sb_kernel, nblk=nblk, scale=hd ** -0.5)
    return pl.pallas_call(
        kern,
        grid=(batch, N_HEADS),
        in_specs=[pl.BlockSpec((seq, hd), lambda b, h: (b, h)),
                  pl.BlockSpec((seq, hd), lambda b, h: (b, h)),
                  pl.BlockSpec((seq, hd), lambda b, h: (b, N_HEADS + h))],
        out_specs=pl.BlockSpec((seq, hd), lambda b, h: (b, h)),
        out_shape=jax.ShapeDtypeStruct((t, d), BF16),
        scratch_shapes=[pltpu.VMEM((hd, seq), BF16),
                        pltpu.VMEM((2, seq, SB_BLOCK), F32),
                        pltpu.VMEM((2, seq, SB_BLOCK), BF16),
                        pltpu.VMEM((nblk, hd, SB_BLOCK), F32),
                        pltpu.VMEM((nblk, SB_BLOCK), F32)],
        compiler_params=_params(2),
        name="stickbreak",
    )(q, kv, kv)


def kernel(x, c, w_ada, b_ada, ln_g, ln_b, w_ffn_in, w_ffn_out, w_qkv_a, w_q_b, w_kv_ada, b_kv_ada, w_kv_b, w_o):
    bsz, seq, d = x.shape
    depth = w_ada.shape[0]
    n_a = w_qkv_a.shape[0]
    alpha = (2.0 * depth) ** 0.25
    t = bsz * seq
    assert seq % MOBA_BLOCK == 0 and seq % SB_BLOCK == 0 and d % (N_HEADS * LANES) == 0

    mod_all = _ada(c, w_ada, b_ada).reshape(depth, bsz, 3 * N_SUB, d)
    kv_mod = _ada(c, w_kv_ada[None], b_kv_ada[None]).reshape(1, bsz, 2, d)

    w_in = w_ffn_in.astype(BF16)
    w_out = w_ffn_out.astype(BF16)
    w_qkv = w_qkv_a.astype(BF16)
    w_q = w_q_b.astype(BF16)
    w_kv = w_kv_b.astype(BF16)[None]
    w_ob = w_o.astype(BF16)
    tabs = _rotary_tables(seq, d // N_HEADS)

    xf = x.reshape(t, d)
    ffn = functools.partial(_ffn, mod_all=mod_all, w_in=w_in, w_out=w_out, ln_g=ln_g, ln_b=ln_b, seq=seq, alpha=alpha)
    kv_proj = (kv_mod, 0, (0, 1), w_kv, 0)
    kv = None
    if n_a == 0 and depth > 0:
        kv = _proj(xf, kv_mod, w_kv, mod_layer=0, w_layer=0, r_shift=0, r_scale=1, seq=seq, name="kv_proj")
    for l in range(depth):
        if l < n_a:
            xf = ffn(xf, layer=l, which=0)
            q, k, v, kbar = _qkv(xf, mod_all, w_qkv, tabs, layer=l, seq=seq)
            att = _moba(q, k, v, kbar.reshape(bsz, seq // MOBA_BLOCK, d), batch=bsz, seq=seq)
        else:
            xf, q = ffn(xf, layer=l, which=0, proj=(mod_all, l, (3, 4), w_q, l - n_a))
            att = _sb(q, kv, batch=bsz, seq=seq)
        if l + 1 == n_a and l + 1 < depth:
            xf, kv = ffn(xf, layer=l, which=1, mixer=(att, w_ob), proj=kv_proj)
        else:
            xf = ffn(xf, layer=l, which=1, mixer=(att, w_ob))
    return xf.reshape(bsz, seq, d)
```

```python
import functools

import jax
import jax.numpy as jnp
from jax import lax
from jax.experimental import pallas as pl
from jax.experimental.pallas import tpu as pltpu

N_HEADS = 8
ROT_FRACTION = 4
ROPE_THETA = 500000.0
MOBA_BLOCK = 256
MOBA_TOPK = 3
SB_BLOCK = 256
SB_NEAR = 2
SB_DEAD_LOG2 = 160.0
LN_EPS = 1e-5
NEG_INF = -1e30
N_SUB = 3
LANES = 128
BF16_SUBLANES = 16
LOG2E = 1.4426950408889634

F32 = jnp.float32
BF16 = jnp.bfloat16

VMEM_LIMIT = 56 * 1024 * 1024
FFN_TILE = 1024


def _params(n_axes):
    return pltpu.CompilerParams(
        dimension_semantics=("parallel",) * n_axes, vmem_limit_bytes=VMEM_LIMIT)


def _dot(a, b):
    return jnp.dot(a, b, preferred_element_type=F32)


def _dot_t(a, b):
    return lax.dot_general(a, b, (((1,), (1,)), ((), ())), preferred_element_type=F32)


def _dot_tn(a, b):
    return lax.dot_general(a, b, (((0,), (0,)), ((), ())), preferred_element_type=F32)


def _layer_norm(r, g, b):
    mu = jnp.mean(r, axis=-1, keepdims=True)
    d = r - mu
    var = jnp.mean(d * d, axis=-1, keepdims=True)
    return d * lax.rsqrt(var + LN_EPS) * g + b


def _ada_kernel(c_ref, w_ref, b_ref, o_ref):
    c = c_ref[...]
    cond = (c * jax.nn.sigmoid(c)).astype(BF16)
    o_ref[...] = _dot(cond, w_ref[...].astype(BF16)) + b_ref[...]


def _ada(c, w, b, tn=1024):
    nl, d, n = w.shape
    bsz = c.shape[0]
    return pl.pallas_call(
        _ada_kernel,
        grid=(nl, n // tn),
        in_specs=[
            pl.BlockSpec((bsz, d), lambda l, j: (0, 0)),
            pl.BlockSpec((None, d, tn), lambda l, j: (l, 0, j)),
            pl.BlockSpec((None, 1, tn), lambda l, j: (l, 0, j)),
        ],
        out_specs=pl.BlockSpec((None, bsz, tn), lambda l, j: (l, 0, j)),
        out_shape=jax.ShapeDtypeStruct((nl, bsz, n), F32),
        compiler_params=_params(2),
        name="ada",
    )(c, w, b.reshape(nl, 1, n))


def _ffn_kernel(*refs, k0, sub, alpha, rows, mixer_out, proj_rows):
    it = iter(refs)
    x_ref, mod_ref = next(it), next(it)
    if mixer_out:
        att_ref, wmix_ref = next(it), next(it)
    wg_ref, wu_ref, wo_ref, lg_ref, lb_ref = (next(it) for _ in range(5))
    if proj_rows:
        pmod_ref, wproj_ref = next(it), next(it)
    o_ref = next(it)
    if proj_rows:
        p_ref = next(it)

    shift = mod_ref[0, k0:k0 + 1, :]
    scale = mod_ref[0, k0 + 1:k0 + 2, :]
    gate = mod_ref[0, k0 + 2:k0 + 3, :]
    def stream_in(r0):
        x = x_ref[r0:r0 + rows, :]
        if mixer_out:
            y = _dot(att_ref[r0:r0 + rows, :], wmix_ref[...])
            x = _layer_norm(alpha * x + (1.0 + mod_ref[0, 5:6, :]) * y, lg_ref[1:2, :], lb_ref[1:2, :])
        return x

    def project(r0, x):
        r_shift, r_scale = proj_rows
        hp = (x * (1.0 + pmod_ref[0, r_scale:r_scale + 1, :]) + pmod_ref[0, r_shift:r_shift + 1, :]).astype(BF16)
        p_ref[r0:r0 + rows, :] = _dot(hp, wproj_ref[...]).astype(p_ref.dtype)

    starts = list(range(0, x_ref.shape[0], rows))
    x_next = stream_in(starts[0])
    project_prev = None
    for n, r0 in enumerate(starts):
        x = x_next
        if n + 1 < len(starts):
            x_next = stream_in(starts[n + 1])
        h = (x * (1.0 + scale) + shift).astype(BF16)
        g = _dot(h, wg_ref[...])
        u = _dot(h, wu_ref[...])
        if project_prev is not None:
            project_prev()
            project_prev = None
        a = (g * jax.nn.sigmoid(g) * u).astype(BF16)
        y = _dot(a, wo_ref[...])
        r = alpha * x + (0.5 * (1.0 + gate)) * y
        x = _layer_norm(r, lg_ref[sub:sub + 1, :], lb_ref[sub:sub + 1, :])
        o_ref[r0:r0 + rows, :] = x
        if proj_rows:
            project_prev = functools.partial(project, r0, x)
    if project_prev is not None:
        project_prev()


def _ffn_tile(d, f, seq, rows, *, with_mixer, n_proj):
    weights = 2 * (3 * d * f + (d * d if with_mixer else 0) + d * n_proj)
    per_token = 2 * (4 * d + 4 * d + (2 * d if with_mixer else 0) + 2 * n_proj)
    temporaries = rows * (2 * f * 4 + f * 2 + 4 * d * 4)
    tm = FFN_TILE
    while tm > rows and (seq % tm or weights + per_token * tm + temporaries > VMEM_LIMIT):
        tm //= 2
    return tm


def _ffn(x, mod_all, w_in, w_out, ln_g, ln_b, *, layer, which, seq, alpha, mixer=None, proj=None, rows=256):
    t, d = x.shape
    f = w_out.shape[2]
    k0 = 0 if which == 0 else 6
    sub = 0 if which == 0 else 2
    n_proj = 0 if proj is None else proj[3].shape[2]
    tm = _ffn_tile(d, f, seq, rows, with_mixer=mixer is not None, n_proj=n_proj)
    per_seq = seq // tm
    tok_spec = pl.BlockSpec((tm, d), lambda i: (i, 0))
    ln_spec = pl.BlockSpec((None, N_SUB, d), lambda i: (layer, 0, 0))
    args = [x, mod_all]
    in_specs = [tok_spec, pl.BlockSpec((None, 1, 3 * N_SUB, d), lambda i: (layer, i // per_seq, 0, 0))]
    if mixer is not None:
        att, w_mix = mixer
        args += [att, w_mix]
        in_specs += [tok_spec, pl.BlockSpec((None, d, d), lambda i: (layer, 0, 0))]
    args += [w_in, w_in, w_out, ln_g, ln_b]
    in_specs += [
        pl.BlockSpec((None, None, d, f), lambda i: (layer, which, 0, 0)),
        pl.BlockSpec((None, None, d, f), lambda i: (layer, which, 0, 1)),
        pl.BlockSpec((None, None, f, d), lambda i: (layer, which, 0, 0)),
        ln_spec, ln_spec,
    ]
    out_specs = [tok_spec]
    out_shape = [jax.ShapeDtypeStruct((t, d), F32)]
    proj_rows = None
    if proj is not None:
        pmod, mod_layer, proj_rows, w_proj, w_layer = proj
        n = w_proj.shape[2]
        args += [pmod, w_proj]
        in_specs += [pl.BlockSpec((None, 1, pmod.shape[2], d), lambda i: (mod_layer, i // per_seq, 0, 0)),
                     pl.BlockSpec((None, d, n), lambda i: (w_layer, 0, 0))]
        out_specs.append(pl.BlockSpec((tm, n), lambda i: (i, 0)))
        out_shape.append(jax.ShapeDtypeStruct((t, n), BF16))
    kern = functools.partial(_ffn_kernel, k0=k0, sub=sub, alpha=alpha, rows=rows,
                             mixer_out=mixer is not None, proj_rows=proj_rows)
    outs = pl.pallas_call(
        kern,
        grid=(t // tm,),
        in_specs=in_specs,
        out_specs=out_specs,
        out_shape=out_shape,
        compiler_params=_params(1),
        name=f"ffn_l{layer}_{which}",
    )(*args)
    return outs if proj is not None else outs[0]


def _rotary_tables(seq, head_dim):
    rot = head_dim // ROT_FRACTION
    half = rot // 2
    inv_freq = jnp.power(ROPE_THETA, -jnp.arange(half, dtype=F32) * 2.0 / rot)
    ang = jnp.arange(seq, dtype=F32)[:, None] * inv_freq[None, :]
    cos, sin = jnp.cos(ang), jnp.sin(ang)
    ones = jnp.ones((seq, head_dim - rot), F32)
    zeros = jnp.zeros((seq, head_dim - rot), F32)
    zh = jnp.zeros((seq, half), F32)
    c_tab = jnp.concatenate([cos, cos, ones], axis=1)
    up_tab = jnp.concatenate([zh, sin, zeros], axis=1)
    dn_tab = jnp.concatenate([-sin, zh, zeros], axis=1)
    return c_tab, up_tab, dn_tab


def _qkv_kernel(x_ref, mod_ref, w_ref, c_ref, up_ref, dn_ref, q_ref, k_ref, v_ref, kbar_ref, *, half):
    x = x_ref[...]
    d = x.shape[1]
    hd = c_ref.shape[1]
    shift = mod_ref[0, 3:4, :]
    scale = mod_ref[0, 4:5, :]
    h = (x * (1.0 + scale) + shift).astype(BF16)
    qkv = _dot(h, w_ref[...])
    c_tab, up_tab, dn_tab = c_ref[...], up_ref[...], dn_ref[...]

    def rotate(t):
        parts = []
        for hh in range(d // hd):
            th = t[:, hh * hd:(hh + 1) * hd]
            parts.append(th * c_tab
                         + pltpu.roll(th, half, 1) * up_tab
                         + pltpu.roll(th, hd - half, 1) * dn_tab)
        return jnp.concatenate(parts, axis=1)

    q = rotate(qkv[:, :d])
    k = rotate(qkv[:, d:2 * d])
    q_ref[...] = q.astype(q_ref.dtype)
    k_ref[...] = k.astype(k_ref.dtype)
    v_ref[...] = qkv[:, 2 * d:].astype(v_ref.dtype)
    for r in range(k.shape[0] // MOBA_BLOCK):
        kbar_ref[0, r:r + 1, :] = jnp.mean(k[r * MOBA_BLOCK:(r + 1) * MOBA_BLOCK], axis=0, keepdims=True)


def _qkv(x, mod_all, w, tabs, *, layer, seq, tm=512):
    t, d = x.shape
    hd = d // N_HEADS
    per_seq = seq // tm
    nb = tm // MOBA_BLOCK
    kern = functools.partial(_qkv_kernel, half=hd // ROT_FRACTION // 2)
    tab_spec = pl.BlockSpec((tm, hd), lambda i: (i % per_seq, 0))
    tok_spec = pl.BlockSpec((tm, d), lambda i: (i, 0))
    return pl.pallas_call(
        kern,
        grid=(t // tm,),
        in_specs=[
            tok_spec,
            pl.BlockSpec((None, 1, 3 * N_SUB, d), lambda i: (layer, i // per_seq, 0, 0)),
            pl.BlockSpec((None, d, 3 * d), lambda i: (layer, 0, 0)),
            tab_spec, tab_spec, tab_spec,
        ],
        out_specs=[tok_spec, tok_spec, tok_spec,
                   pl.BlockSpec((1, nb, d), lambda i: (i, 0, 0))],
        out_shape=[jax.ShapeDtypeStruct((t, d), BF16)] * 3
        + [jax.ShapeDtypeStruct((t // tm, nb, d), F32)],
        compiler_params=_params(1),
        name=f"qkv_l{layer}",
    )(x, mod_all, w, *tabs)


def _proj_kernel(x_ref, mod_ref, w_ref, o_ref, *, r_shift, r_scale):
    x = x_ref[...]
    shift = mod_ref[0, r_shift:r_shift + 1, :]
    scale = mod_ref[0, r_scale:r_scale + 1, :]
    h = (x * (1.0 + scale) + shift).astype(BF16)
    o_ref[...] = _dot(h, w_ref[...]).astype(o_ref.dtype)


def _proj(x, mod, w, *, mod_layer, w_layer, r_shift, r_scale, seq, tm=512, name="proj"):
    t, d = x.shape
    n = w.shape[2]
    rows = mod.shape[2]
    per_seq = seq // tm
    kern = functools.partial(_proj_kernel, r_shift=r_shift, r_scale=r_scale)
    return pl.pallas_call(
        kern,
        grid=(t // tm,),
        in_specs=[
            pl.BlockSpec((tm, d), lambda i: (i, 0)),
            pl.BlockSpec((None, 1, rows, d), lambda i: (mod_layer, i // per_seq, 0, 0)),
            pl.BlockSpec((None, d, n), lambda i: (w_layer, 0, 0)),
        ],
        out_specs=pl.BlockSpec((tm, n), lambda i: (i, 0)),
        out_shape=jax.ShapeDtypeStruct((t, n), BF16),
        compiler_params=_params(1),
        name=name,
    )(x, mod, w)


def _moba_kernel(q_ref, k_ref, v_ref, kbar_ref, o_ref, vt_ref, s_ref, p_ref, *, nblk, scale):
    blk = MOBA_BLOCK
    hd = q_ref.shape[1]
    grows = -(-nblk // BF16_SUBLANES) * BF16_SUBLANES
    kb = jnp.concatenate([kbar_ref[0], jnp.zeros((grows - nblk, hd), F32)], axis=0)
    kb_hi = kb.astype(BF16)
    kb_lo = (kb - kb_hi.astype(F32)).astype(BF16)
    key = lax.broadcasted_iota(jnp.int32, (blk, blk), 0)
    qry = lax.broadcasted_iota(jnp.int32, (blk, blk), 1)
    causal = key <= qry
    grow = lax.broadcasted_iota(jnp.int32, (grows, blk), 0).astype(F32)
    c2 = scale * LOG2E

    for j in range(nblk):
        vt_ref[:, j * blk:(j + 1) * blk] = v_ref[j * blk:(j + 1) * blk, :].astype(F32).T.astype(BF16)

    def select_blocks(i):
        if i <= MOBA_TOPK:
            return None
        qi = q_ref[i * blk:(i + 1) * blk, :]
        gate = _dot_t(kb_hi, qi) + _dot_t(kb_lo, qi)
        gate = jnp.where(grow < float(i), gate, -jnp.inf)
        sel = jnp.full((grows, blk), NEG_INF, F32)
        for _ in range(MOBA_TOPK):
            top = jnp.max(gate, axis=0, keepdims=True)
            first = jnp.min(jnp.where(gate == top, grow, float(grows)), axis=0, keepdims=True)
            hit = grow == first
            sel = jnp.where(hit, 0.0, sel)
            gate = jnp.where(hit, -jnp.inf, gate)
        return sel

    def scores(i, j, sel, m):
        s = _dot_t(k_ref[j * blk:(j + 1) * blk, :], q_ref[i * blk:(i + 1) * blk, :]) * c2
        s_ref[i % 2, j * blk:(j + 1) * blk, :] = s
        if j == i:
            s = jnp.where(causal, s, NEG_INF)
        elif sel is not None:
            s = s + sel[j:j + 1, :]
        top = jnp.max(s, axis=0, keepdims=True)
        return top if m is None else jnp.maximum(m, top)

    def weighted_values(i, l):
        acc = _dot(vt_ref[:, 0:(i + 1) * blk], p_ref[i % 2, 0:(i + 1) * blk, :])
        o_ref[i * blk:(i + 1) * blk, :] = (acc * (1.0 / l)).T.astype(o_ref.dtype)

    sel, m = None, scores(0, 0, None, None)
    l_prev = None
    for i in range(nblk):
        if i + 1 < nblk:
            sel_next = select_blocks(i + 1)
        if i > 0:
            weighted_values(i - 1, l_prev)
        m_next = None
        l = jnp.zeros((1, blk), F32)
        for j in range(i + 1):
            if i + 1 < nblk:
                m_next = scores(i + 1, j, sel_next, m_next)
            s = s_ref[i % 2, j * blk:(j + 1) * blk, :]
            if j == i:
                p = jnp.exp2(jnp.where(causal, s, NEG_INF) - m)
            else:
                p = jnp.exp2(s + ((-m) if sel is None else (sel[j:j + 1, :] - m)))
            l = l + jnp.sum(p, axis=0, keepdims=True)
            p_ref[i % 2, j * blk:(j + 1) * blk, :] = p.astype(BF16)
        if i + 1 < nblk:
            m_next = scores(i + 1, i + 1, sel_next, m_next)
            sel, m = sel_next, m_next
        l_prev = l
    weighted_values(nblk - 1, l_prev)


def _moba(q, k, v, kbar, *, batch, seq):
    t, d = q.shape
    hd = d // N_HEADS
    nblk = seq // MOBA_BLOCK
    kern = functools.partial(_moba_kernel, nblk=nblk, scale=hd ** -0.5)
    head_spec = pl.BlockSpec((seq, hd), lambda b, h: (b, h))
    return pl.pallas_call(
        kern,
        grid=(batch, N_HEADS),
        in_specs=[head_spec, head_spec, head_spec,
                  pl.BlockSpec((1, nblk, hd), lambda b, h: (b, 0, h))],
        out_specs=head_spec,
        out_shape=jax.ShapeDtypeStruct((t, d), BF16),
        scratch_shapes=[pltpu.VMEM((hd, seq), BF16),
                        pltpu.VMEM((2, seq, MOBA_BLOCK), F32),
                        pltpu.VMEM((2, seq, MOBA_BLOCK), BF16)],
        compiler_params=_params(2),
        name="moba",
    )(q, k, v, kbar)


def _softplus2(z2):
    neg_abs = lax.bitcast_convert_type(
        lax.bitcast_convert_type(z2, jnp.uint32) | jnp.uint32(0x80000000), F32)
    return jnp.maximum(z2, 0.0) + jnp.log2(1.0 + jnp.exp2(neg_abs))


def _sb_kernel(q_ref, k_ref, v_ref, o_ref, vt_ref, z_ref, w_ref, acc_ref, tail_ref, *, nblk, scale):
    blk = SB_BLOCK
    key = lax.broadcasted_iota(jnp.int32, (blk, blk), 0)
    qry = lax.broadcasted_iota(jnp.int32, (blk, blk), 1)
    strict = key < qry
    later = (qry > key).astype(BF16)
    later2 = jnp.concatenate([later, later], axis=1)
    c2 = scale * LOG2E

    def sum_later(sp):
        hi = sp.astype(BF16)
        lo = (sp - hi.astype(F32)).astype(BF16)
        return _dot(later2, jnp.concatenate([hi, lo], axis=0))

    for j in range(nblk):
        vt_ref[:, j * blk:(j + 1) * blk] = v_ref[j * blk:(j + 1) * blk, :].astype(F32).T.astype(BF16)

    def logits(i, b0, b1):
        z_ref[i % 2, b0 * blk:b1 * blk, :] = _dot_t(k_ref[b0 * blk:b1 * blk, :], q_ref[i * blk:(i + 1) * blk, :]) * c2

    def block_weights(i, j, tail):
        z = z_ref[i % 2, j * blk:(j + 1) * blk, :]
        sp = _softplus2(z)
        if j == i:
            sp = jnp.where(strict, sp, 0.0)
        e = z - sp - sum_later(sp)
        w = jnp.exp2(e if tail is None else e - tail)
        if j == i:
            w = jnp.where(strict, w, 0.0)
        w_ref[i % 2, j * blk:(j + 1) * blk, :] = w.astype(BF16)
        if j == 0:
            return tail
        part = jnp.sum(sp, axis=0, keepdims=True)
        return part if tail is None else tail + part

    def near_values(i):
        b0 = max(i + 1 - SB_NEAR, 0)
        acc = _dot(vt_ref[:, b0 * blk:(i + 1) * blk], w_ref[i % 2, b0 * blk:(i + 1) * blk, :])
        if b0 > 0:
            acc_ref[i] = acc
        o_ref[i * blk:(i + 1) * blk, :] = acc.T.astype(o_ref.dtype)

    logits(0, 0, 1)
    least_tail = None
    for i in range(nblk):
        if i + 1 < nblk:
            logits(i + 1, max(i + 2 - SB_NEAR, 0), i + 2)
        if i > 0:
            near_values(i - 1)
        b0 = max(i + 1 - SB_NEAR, 0)
        tail = None
        for j in range(i, b0 - 1, -1):
            tail = block_weights(i, j, tail)
        if b0 > 0:
            tail_ref[i:i + 1, :] = tail
            least_tail = tail if least_tail is None else jnp.minimum(least_tail, tail)
    near_values(nblk - 1)

    if nblk > SB_NEAR:
        @pl.when(jnp.min(least_tail) < SB_DEAD_LOG2)
        def _():
            for i in range(SB_NEAR, nblk):
                b1 = i + 1 - SB_NEAR
                logits(i, 0, b1)
                tail = tail_ref[i:i + 1, :]
                for j in range(b1 - 1, -1, -1):
                    tail = block_weights(i, j, tail)
                acc = acc_ref[i] + _dot(vt_ref[:, 0:b1 * blk], w_ref[i % 2, 0:b1 * blk, :])
                o_ref[i * blk:(i + 1) * blk, :] = acc.T.astype(o_ref.dtype)


def _sb(q, kv, *, batch, seq):
    t, d = q.shape
    hd = d // N_HEADS
    nblk = seq // SB_BLOCK
    kern = functools.partial(_sb_kernel, nblk=nblk, scale=hd ** -0.5)
    return pl.pallas_call(
        kern,
        grid=(batch, N_HEADS),
        in_specs=[pl.BlockSpec((seq, hd), lambda b, h: (b, h)),
                  pl.BlockSpec((seq, hd), lambda b, h: (b, h)),
                  pl.BlockSpec((seq, hd), lambda b, h: (b, N_HEADS + h))],
        out_specs=pl.BlockSpec((seq, hd), lambda b, h: (b, h)),
        out_shape=jax.ShapeDtypeStruct((t, d), BF16),
        scratch_shapes=[pltpu.VMEM((hd, seq), BF16),
                        pltpu.VMEM((2, seq, SB_BLOCK), F32),
                        pltpu.VMEM((2, seq, SB_BLOCK), BF16),
                        pltpu.VMEM((nblk, hd, SB_BLOCK), F32),
                        pltpu.VMEM((nblk, SB_BLOCK), F32)],
        compiler_params=_params(2),
        name="stickbreak",
    )(q, kv, kv)


def kernel(x, c, w_ada, b_ada, ln_g, ln_b, w_ffn_in, w_ffn_out, w_qkv_a, w_q_b, w_kv_ada, b_kv_ada, w_kv_b, w_o):
    bsz, seq, d = x.shape
    depth = w_ada.shape[0]
    n_a = w_qkv_a.shape[0]
    alpha = (2.0 * depth) ** 0.25
    t = bsz * seq
    assert seq % MOBA_BLOCK == 0 and seq % SB_BLOCK == 0 and d % (N_HEADS * LANES) == 0

    mod_all = _ada(c, w_ada, b_ada).reshape(depth, bsz, 3 * N_SUB, d)
    kv_mod = _ada(c, w_kv_ada[None], b_kv_ada[None]).reshape(1, bsz, 2, d)

    w_in = w_ffn_in.astype(BF16)
    w_out = w_ffn_out.astype(BF16)
    w_qkv = w_qkv_a.astype(BF16)
    w_q = w_q_b.astype(BF16)
    w_kv = w_kv_b.astype(BF16)[None]
    w_ob = w_o.astype(BF16)
    tabs = _rotary_tables(seq, d // N_HEADS)

    xf = x.reshape(t, d)
    ffn = functools.partial(_ffn, mod_all=mod_all, w_in=w_in, w_out=w_out, ln_g=ln_g, ln_b=ln_b, seq=seq, alpha=alpha)
    kv_proj = (kv_mod, 0, (0, 1), w_kv, 0)
    kv = None
    if n_a == 0 and depth > 0:
        kv = _proj(xf, kv_mod, w_kv, mod_layer=0, w_layer=0, r_shift=0, r_scale=1, seq=seq, name="kv_proj")
    for l in range(depth):
        if l < n_a:
            xf = ffn(xf, layer=l, which=0)
            q, k, v, kbar = _qkv(xf, mod_all, w_qkv, tabs, layer=l, seq=seq)
            att = _moba(q, k, v, kbar.reshape(bsz, seq // MOBA_BLOCK, d), batch=bsz, seq=seq)
        else:
            xf, q = ffn(xf, layer=l, which=0, proj=(mod_all, l, (3, 4), w_q, l - n_a))
            att = _sb(q, kv, batch=bsz, seq=seq)
        if l + 1 == n_a and l + 1 < depth:
            xf, kv = ffn(xf, layer=l, which=1, mixer=(att, w_ob), proj=kv_proj)
        else:
            xf = ffn(xf, layer=l, which=1, mixer=(att, w_ob))
    return xf.reshape(bsz, seq, d)
```

```python
import functools

import jax
import jax.numpy as jnp
from jax import lax
from jax.experimental import pallas as pl
from jax.experimental.pallas import tpu as pltpu

N_HEADS = 8
ATT_HEADS_PER_STEP = 4
ROT_FRACTION = 4
ROPE_THETA = 500000.0
MOBA_BLOCK = 256
MOBA_TOPK = 3
SB_BLOCK = 256
SB_NEAR = 2
SB_DEAD_LOG2 = 160.0
LN_EPS = 1e-5
NEG_INF = -1e30
N_SUB = 3
LANES = 128
BF16_SUBLANES = 16
LOG2E = 1.4426950408889634

F32 = jnp.float32
BF16 = jnp.bfloat16

VMEM_LIMIT = 56 * 1024 * 1024
FFN_TILE = 1024


def _params(n_axes):
    return pltpu.CompilerParams(
        dimension_semantics=("parallel",) * n_axes, vmem_limit_bytes=VMEM_LIMIT)


def _dot(a, b):
    return jnp.dot(a, b, preferred_element_type=F32)


def _dot_t(a, b):
    return lax.dot_general(a, b, (((1,), (1,)), ((), ())), preferred_element_type=F32)


def _dot_tn(a, b):
    return lax.dot_general(a, b, (((0,), (0,)), ((), ())), preferred_element_type=F32)


def _layer_norm(r, g, b):
    mu = jnp.mean(r, axis=-1, keepdims=True)
    d = r - mu
    var = jnp.mean(d * d, axis=-1, keepdims=True)
    return d * lax.rsqrt(var + LN_EPS) * g + b


def _ada_kernel(c_ref, w_ref, b_ref, o_ref):
    c = c_ref[...]
    cond = (c * jax.nn.sigmoid(c)).astype(BF16)
    o_ref[...] = _dot(cond, w_ref[...].astype(BF16)) + b_ref[...]


def _ada(c, w, b, tn=1024):
    nl, d, n = w.shape
    bsz = c.shape[0]
    return pl.pallas_call(
        _ada_kernel,
        grid=(nl, n // tn),
        in_specs=[
            pl.BlockSpec((bsz, d), lambda l, j: (0, 0)),
            pl.BlockSpec((None, d, tn), lambda l, j: (l, 0, j)),
            pl.BlockSpec((None, 1, tn), lambda l, j: (l, 0, j)),
        ],
        out_specs=pl.BlockSpec((None, bsz, tn), lambda l, j: (l, 0, j)),
        out_shape=jax.ShapeDtypeStruct((nl, bsz, n), F32),
        compiler_params=_params(2),
        name="ada",
    )(c, w, b.reshape(nl, 1, n))


def _ffn_kernel(*refs, k0, sub, alpha, rows, mixer_out, proj_rows):
    it = iter(refs)
    x_ref, mod_ref = next(it), next(it)
    if mixer_out:
        att_ref, wmix_ref = next(it), next(it)
    wg_ref, wu_ref, wo_ref, lg_ref, lb_ref = (next(it) for _ in range(5))
    if proj_rows:
        pmod_ref, wproj_ref = next(it), next(it)
    o_ref = next(it)
    if proj_rows:
        p_ref = next(it)

    shift = mod_ref[0, k0:k0 + 1, :]
    scale = mod_ref[0, k0 + 1:k0 + 2, :]
    gate = mod_ref[0, k0 + 2:k0 + 3, :]
    def stream_in(r0):
        x = x_ref[r0:r0 + rows, :]
        if mixer_out:
            y = _dot(att_ref[r0:r0 + rows, :], wmix_ref[...])
            x = _layer_norm(alpha * x + (1.0 + mod_ref[0, 5:6, :]) * y, lg_ref[1:2, :], lb_ref[1:2, :])
        return x

    def project(r0, x):
        r_shift, r_scale = proj_rows
        hp = (x * (1.0 + pmod_ref[0, r_scale:r_scale + 1, :]) + pmod_ref[0, r_shift:r_shift + 1, :]).astype(BF16)
        p_ref[r0:r0 + rows, :] = _dot(hp, wproj_ref[...]).astype(p_ref.dtype)

    starts = list(range(0, x_ref.shape[0], rows))
    x_next = stream_in(starts[0])
    project_prev = None
    for n, r0 in enumerate(starts):
        x = x_next
        if n + 1 < len(starts):
            x_next = stream_in(starts[n + 1])
        h = (x * (1.0 + scale) + shift).astype(BF16)
        g = _dot(h, wg_ref[...])
        u = _dot(h, wu_ref[...])
        if project_prev is not None:
            project_prev()
            project_prev = None
        a = (g * jax.nn.sigmoid(g) * u).astype(BF16)
        y = _dot(a, wo_ref[...])
        r = alpha * x + (0.5 * (1.0 + gate)) * y
        x = _layer_norm(r, lg_ref[sub:sub + 1, :], lb_ref[sub:sub + 1, :])
        o_ref[r0:r0 + rows, :] = x
        if proj_rows:
            project_prev = functools.partial(project, r0, x)
    if project_prev is not None:
        project_prev()


def _ffn_tile(d, f, seq, rows, *, with_mixer, n_proj):
    weights = 2 * (3 * d * f + (d * d if with_mixer else 0) + d * n_proj)
    per_token = 2 * (4 * d + 4 * d + (2 * d if with_mixer else 0) + 2 * n_proj)
    temporaries = rows * (2 * f * 4 + f * 2 + 4 * d * 4)
    tm = FFN_TILE
    while tm > rows and (seq % tm or weights + per_token * tm + temporaries > VMEM_LIMIT):
        tm //= 2
    return tm


def _ffn(x, mod_all, w_in, w_out, ln_g, ln_b, *, layer, which, seq, alpha, mixer=None, proj=None, rows=256):
    t, d = x.shape
    f = w_out.shape[2]
    k0 = 0 if which == 0 else 6
    sub = 0 if which == 0 else 2
    n_proj = 0 if proj is None else proj[3].shape[2]
    tm = _ffn_tile(d, f, seq, rows, with_mixer=mixer is not None, n_proj=n_proj)
    per_seq = seq // tm
    tok_spec = pl.BlockSpec((tm, d), lambda i: (i, 0))
    ln_spec = pl.BlockSpec((None, N_SUB, d), lambda i: (layer, 0, 0))
    args = [x, mod_all]
    in_specs = [tok_spec, pl.BlockSpec((None, 1, 3 * N_SUB, d), lambda i: (layer, i // per_seq, 0, 0))]
    if mixer is not None:
        att, w_mix = mixer
        args += [att, w_mix]
        in_specs += [tok_spec, pl.BlockSpec((None, d, d), lambda i: (layer, 0, 0))]
    args += [w_in, w_in, w_out, ln_g, ln_b]
    in_specs += [
        pl.BlockSpec((None, None, d, f), lambda i: (layer, which, 0, 0)),
        pl.BlockSpec((None, None, d, f), lambda i: (layer, which, 0, 1)),
        pl.BlockSpec((None, None, f, d), lambda i: (layer, which, 0, 0)),
        ln_spec, ln_spec,
    ]
    out_specs = [tok_spec]
    out_shape = [jax.ShapeDtypeStruct((t, d), F32)]
    proj_rows = None
    if proj is not None:
        pmod, mod_layer, proj_rows, w_proj, w_layer = proj
        n = w_proj.shape[2]
        args += [pmod, w_proj]
        in_specs += [pl.BlockSpec((None, 1, pmod.shape[2], d), lambda i: (mod_layer, i // per_seq, 0, 0)),
                     pl.BlockSpec((None, d, n), lambda i: (w_layer, 0, 0))]
        out_specs.append(pl.BlockSpec((tm, n), lambda i: (i, 0)))
        out_shape.append(jax.ShapeDtypeStruct((t, n), BF16))
    kern = functools.partial(_ffn_kernel, k0=k0, sub=sub, alpha=alpha, rows=rows,
                             mixer_out=mixer is not None, proj_rows=proj_rows)
    outs = pl.pallas_call(
        kern,
        grid=(t // tm,),
        in_specs=in_specs,
        out_specs=out_specs,
        out_shape=out_shape,
        compiler_params=_params(1),
        name=f"ffn_l{layer}_{which}",
    )(*args)
    return outs if proj is not None else outs[0]


def _rotary_tables(seq, head_dim):
    rot = head_dim // ROT_FRACTION
    half = rot // 2
    inv_freq = jnp.power(ROPE_THETA, -jnp.arange(half, dtype=F32) * 2.0 / rot)
    ang = jnp.arange(seq, dtype=F32)[:, None] * inv_freq[None, :]
    cos, sin = jnp.cos(ang), jnp.sin(ang)
    ones = jnp.ones((seq, head_dim - rot), F32)
    zeros = jnp.zeros((seq, head_dim - rot), F32)
    zh = jnp.zeros((seq, half), F32)
    c_tab = jnp.concatenate([cos, cos, ones], axis=1)
    up_tab = jnp.concatenate([zh, sin, zeros], axis=1)
    dn_tab = jnp.concatenate([-sin, zh, zeros], axis=1)
    return c_tab, up_tab, dn_tab


def _qkv_kernel(x_ref, mod_ref, w_ref, c_ref, up_ref, dn_ref, q_ref, k_ref, v_ref, kbar_ref, *, half):
    x = x_ref[...]
    d = x.shape[1]
    hd = c_ref.shape[1]
    shift = mod_ref[0, 3:4, :]
    scale = mod_ref[0, 4:5, :]
    h = (x * (1.0 + scale) + shift).astype(BF16)
    qkv = _dot(h, w_ref[...])
    c_tab, up_tab, dn_tab = c_ref[...], up_ref[...], dn_ref[...]

    def rotate(t):
        parts = []
        for hh in range(d // hd):
            th = t[:, hh * hd:(hh + 1) * hd]
            parts.append(th * c_tab
                         + pltpu.roll(th, half, 1) * up_tab
                         + pltpu.roll(th, hd - half, 1) * dn_tab)
        return jnp.concatenate(parts, axis=1)

    q = rotate(qkv[:, :d])
    k = rotate(qkv[:, d:2 * d])
    q_ref[...] = q.astype(q_ref.dtype)
    k_ref[...] = k.astype(k_ref.dtype)
    v_ref[...] = qkv[:, 2 * d:].astype(v_ref.dtype)
    for r in range(k.shape[0] // MOBA_BLOCK):
        kbar_ref[0, r:r + 1, :] = jnp.mean(k[r * MOBA_BLOCK:(r + 1) * MOBA_BLOCK], axis=0, keepdims=True)


def _qkv(x, mod_all, w, tabs, *, layer, seq, tm=512):
    t, d = x.shape
    hd = d // N_HEADS
    per_seq = seq // tm
    nb = tm // MOBA_BLOCK
    kern = functools.partial(_qkv_kernel, half=hd // ROT_FRACTION // 2)
    tab_spec = pl.BlockSpec((tm, hd), lambda i: (i % per_seq, 0))
    tok_spec = pl.BlockSpec((tm, d), lambda i: (i, 0))
    return pl.pallas_call(
        kern,
        grid=(t // tm,),
        in_specs=[
            tok_spec,
            pl.BlockSpec((None, 1, 3 * N_SUB, d), lambda i: (layer, i // per_seq, 0, 0)),
            pl.BlockSpec((None, d, 3 * d), lambda i: (layer, 0, 0)),
            tab_spec, tab_spec, tab_spec,
        ],
        out_specs=[tok_spec, tok_spec, tok_spec,
                   pl.BlockSpec((1, nb, d), lambda i: (i, 0, 0))],
        out_shape=[jax.ShapeDtypeStruct((t, d), BF16)] * 3
        + [jax.ShapeDtypeStruct((t // tm, nb, d), F32)],
        compiler_params=_params(1),
        name=f"qkv_l{layer}",
    )(x, mod_all, w, *tabs)


def _proj_kernel(x_ref, mod_ref, w_ref, o_ref, *, r_shift, r_scale):
    x = x_ref[...]
    shift = mod_ref[0, r_shift:r_shift + 1, :]
    scale = mod_ref[0, r_scale:r_scale + 1, :]
    h = (x * (1.0 + scale) + shift).astype(BF16)
    o_ref[...] = _dot(h, w_ref[...]).astype(o_ref.dtype)


def _proj(x, mod, w, *, mod_layer, w_layer, r_shift, r_scale, seq, tm=512, name="proj"):
    t, d = x.shape
    n = w.shape[2]
    rows = mod.shape[2]
    per_seq = seq // tm
    kern = functools.partial(_proj_kernel, r_shift=r_shift, r_scale=r_scale)
    return pl.pallas_call(
        kern,
        grid=(t // tm,),
        in_specs=[
            pl.BlockSpec((tm, d), lambda i: (i, 0)),
            pl.BlockSpec((None, 1, rows, d), lambda i: (mod_layer, i // per_seq, 0, 0)),
            pl.BlockSpec((None, d, n), lambda i: (w_layer, 0, 0)),
        ],
        out_specs=pl.BlockSpec((tm, n), lambda i: (i, 0)),
        out_shape=jax.ShapeDtypeStruct((t, n), BF16),
        compiler_params=_params(1),
        name=name,
    )(x, mod, w)


def _per_head(head_fn, hd, io_refs, scratch_refs, **kw):
    for h0 in range(0, io_refs[0].shape[-1], hd):
        views = [r.at[(slice(None),) * (len(r.shape) - 1) + (slice(h0, h0 + hd),)] for r in io_refs]
        head_fn(*views, *scratch_refs, **kw)


def _moba_kernel(q_ref, k_ref, v_ref, kbar_ref, o_ref, *scratch, hd, **kw):
    _per_head(_moba_head, hd, (q_ref, k_ref, v_ref, kbar_ref, o_ref), scratch, **kw)


def _moba_head(q_ref, k_ref, v_ref, kbar_ref, o_ref, vt_ref, s_ref, p_ref, *, nblk, scale):
    blk = MOBA_BLOCK
    hd = q_ref.shape[1]
    grows = -(-nblk // BF16_SUBLANES) * BF16_SUBLANES
    kb = jnp.concatenate([kbar_ref[0], jnp.zeros((grows - nblk, hd), F32)], axis=0)
    kb_hi = kb.astype(BF16)
    kb_lo = (kb - kb_hi.astype(F32)).astype(BF16)
    key = lax.broadcasted_iota(jnp.int32, (blk, blk), 0)
    qry = lax.broadcasted_iota(jnp.int32, (blk, blk), 1)
    causal = key <= qry
    grow = lax.broadcasted_iota(jnp.int32, (grows, blk), 0).astype(F32)
    c2 = scale * LOG2E

    for j in range(nblk):
        vt_ref[:, j * blk:(j + 1) * blk] = v_ref[j * blk:(j + 1) * blk, :].astype(F32).T.astype(BF16)

    def select_blocks(i):
        if i <= MOBA_TOPK:
            return None
        qi = q_ref[i * blk:(i + 1) * blk, :]
        gate = _dot_t(kb_hi, qi) + _dot_t(kb_lo, qi)
        gate = jnp.where(grow < float(i), gate, -jnp.inf)
        sel = jnp.full((grows, blk), NEG_INF, F32)
        for _ in range(MOBA_TOPK):
            top = jnp.max(gate, axis=0, keepdims=True)
            first = jnp.min(jnp.where(gate == top, grow, float(grows)), axis=0, keepdims=True)
            hit = grow == first
            sel = jnp.where(hit, 0.0, sel)
            gate = jnp.where(hit, -jnp.inf, gate)
        return sel

    def scores(i, j, sel, m):
        s = _dot_t(k_ref[j * blk:(j + 1) * blk, :], q_ref[i * blk:(i + 1) * blk, :]) * c2
        s_ref[i % 2, j * blk:(j + 1) * blk, :] = s
        if j == i:
            s = jnp.where(causal, s, NEG_INF)
        elif sel is not None:
            s = s + sel[j:j + 1, :]
        top = jnp.max(s, axis=0, keepdims=True)
        return top if m is None else jnp.maximum(m, top)

    def weighted_values(i, l):
        acc = _dot(vt_ref[:, 0:(i + 1) * blk], p_ref[i % 2, 0:(i + 1) * blk, :])
        o_ref[i * blk:(i + 1) * blk, :] = (acc * (1.0 / l)).T.astype(o_ref.dtype)

    sel, m = None, scores(0, 0, None, None)
    l_prev = None
    for i in range(nblk):
        if i + 1 < nblk:
            sel_next = select_blocks(i + 1)
        if i > 0:
            weighted_values(i - 1, l_prev)
        m_next = None
        l = jnp.zeros((1, blk), F32)
        for j in range(i + 1):
            if i + 1 < nblk:
                m_next = scores(i + 1, j, sel_next, m_next)
            s = s_ref[i % 2, j * blk:(j + 1) * blk, :]
            if j == i:
                p = jnp.exp2(jnp.where(causal, s, NEG_INF) - m)
            else:
                p = jnp.exp2(s + ((-m) if sel is None else (sel[j:j + 1, :] - m)))
            l = l + jnp.sum(p, axis=0, keepdims=True)
            p_ref[i % 2, j * blk:(j + 1) * blk, :] = p.astype(BF16)
        if i + 1 < nblk:
            m_next = scores(i + 1, i + 1, sel_next, m_next)
            sel, m = sel_next, m_next
        l_prev = l
    weighted_values(nblk - 1, l_prev)


def _moba(q, k, v, kbar, *, batch, seq):
    t, d = q.shape
    hd = d // N_HEADS
    nblk = seq // MOBA_BLOCK
    kern = functools.partial(_moba_kernel, hd=hd, nblk=nblk, scale=hd ** -0.5)
    width = ATT_HEADS_PER_STEP * hd
    head_spec = pl.BlockSpec((seq, width), lambda b, g: (b, g))
    return pl.pallas_call(
        kern,
        grid=(batch, N_HEADS // ATT_HEADS_PER_STEP),
        in_specs=[head_spec, head_spec, head_spec,
                  pl.BlockSpec((1, nblk, width), lambda b, g: (b, 0, g))],
        out_specs=head_spec,
        out_shape=jax.ShapeDtypeStruct((t, d), BF16),
        scratch_shapes=[pltpu.VMEM((hd, seq), BF16),
                        pltpu.VMEM((2, seq, MOBA_BLOCK), F32),
                        pltpu.VMEM((2, seq, MOBA_BLOCK), BF16)],
        compiler_params=_params(2),
        name="moba",
    )(q, k, v, kbar)


def _softplus2(z2):
    neg_abs = lax.bitcast_convert_type(
        lax.bitcast_convert_type(z2, jnp.uint32) | jnp.uint32(0x80000000), F32)
    return jnp.maximum(z2, 0.0) + jnp.log2(1.0 + jnp.exp2(neg_abs))


def _sb_kernel(q_ref, k_ref, v_ref, o_ref, *scratch, hd, **kw):
    _per_head(_sb_head, hd, (q_ref, k_ref, v_ref, o_ref), scratch, **kw)


def _sb_head(q_ref, k_ref, v_ref, o_ref, vt_ref, z_ref, w_ref, acc_ref, tail_ref, *, nblk, scale):
    blk = SB_BLOCK
    key = lax.broadcasted_iota(jnp.int32, (blk, blk), 0)
    qry = lax.broadcasted_iota(jnp.int32, (blk, blk), 1)
    strict = key < qry
    later = (qry > key).astype(BF16)
    later2 = jnp.concatenate([later, later], axis=1)
    c2 = scale * LOG2E

    def sum_later(sp):
        hi = sp.astype(BF16)
        lo = (sp - hi.astype(F32)).astype(BF16)
        return _dot(later2, jnp.concatenate([hi, lo], axis=0))

    for j in range(nblk):
        vt_ref[:, j * blk:(j + 1) * blk] = v_ref[j * blk:(j + 1) * blk, :].astype(F32).T.astype(BF16)

    def logits(i, b0, b1):
        z_ref[i % 2, b0 * blk:b1 * blk, :] = _dot_t(k_ref[b0 * blk:b1 * blk, :], q_ref[i * blk:(i + 1) * blk, :]) * c2

    def block_weights(i, j, tail):
        z = z_ref[i % 2, j * blk:(j + 1) * blk, :]
        sp = _softplus2(z)
        if j == i:
            sp = jnp.where(strict, sp, 0.0)
        e = z - sp - sum_later(sp)
        w = jnp.exp2(e if tail is None else e - tail)
        if j == i:
            w = jnp.where(strict, w, 0.0)
        w_ref[i % 2, j * blk:(j + 1) * blk, :] = w.astype(BF16)
        if j == 0:
            return tail
        part = jnp.sum(sp, axis=0, keepdims=True)
        return part if tail is None else tail + part

    def near_values(i):
        b0 = max(i + 1 - SB_NEAR, 0)
        acc = _dot(vt_ref[:, b0 * blk:(i + 1) * blk], w_ref[i % 2, b0 * blk:(i + 1) * blk, :])
        if b0 > 0:
            acc_ref[i] = acc
        o_ref[i * blk:(i + 1) * blk, :] = acc.T.astype(o_ref.dtype)

    logits(0, 0, 1)
    least_tail = None
    for i in range(nblk):
        if i + 1 < nblk:
            logits(i + 1, max(i + 2 - SB_NEAR, 0), i + 2)
        if i > 0:
            near_values(i - 1)
        b0 = max(i + 1 - SB_NEAR, 0)
        tail = None
        for j in range(i, b0 - 1, -1):
            tail = block_weights(i, j, tail)
        if b0 > 0:
            tail_ref[i:i + 1, :] = tail
            least_tail = tail if least_tail is None else jnp.minimum(least_tail, tail)
    near_values(nblk - 1)

    if nblk > SB_NEAR:
        @pl.when(jnp.min(least_tail) < SB_DEAD_LOG2)
        def _():
            for i in range(SB_NEAR, nblk):
                b1 = i + 1 - SB_NEAR
                logits(i, 0, b1)
                tail = tail_ref[i:i + 1, :]
                for j in range(b1 - 1, -1, -1):
                    tail = block_weights(i, j, tail)
                acc = acc_ref[i] + _dot(vt_ref[:, 0:b1 * blk], w_ref[i % 2, 0:b1 * blk, :])
                o_ref[i * blk:(i + 1) * blk, :] = acc.T.astype(o_ref.dtype)


def _sb(q, kv, *, batch, seq):
    t, d = q.shape
    hd = d // N_HEADS
    nblk = seq // SB_BLOCK
    kern = functools.partial(_sb_kernel, hd=hd, nblk=nblk, scale=hd ** -0.5)
    groups = N_HEADS // ATT_HEADS_PER_STEP
    width = ATT_HEADS_PER_STEP * hd
    return pl.pallas_call(
        kern,
        grid=(batch, groups),
        in_specs=[pl.BlockSpec((seq, width), lambda b, g: (b, g)),
                  pl.BlockSpec((seq, width), lambda b, g: (b, g)),
                  pl.BlockSpec((seq, width), lambda b, g: (b, groups + g))],
        out_specs=pl.BlockSpec((seq, width), lambda b, g: (b, g)),
        out_shape=jax.ShapeDtypeStruct((t, d), BF16),
        scratch_shapes=[pltpu.VMEM((hd, seq), BF16),
                        pltpu.VMEM((2, seq, SB_BLOCK), F32),
                        pltpu.VMEM((2, seq, SB_BLOCK), BF16),
                        pltpu.VMEM((nblk, hd, SB_BLOCK), F32),
                        pltpu.VMEM((nblk, SB_BLOCK), F32)],
        compiler_params=_params(2),
        name="stickbreak",
    )(q, kv, kv)


def kernel(x, c, w_ada, b_ada, ln_g, ln_b, w_ffn_in, w_ffn_out, w_qkv_a, w_q_b, w_kv_ada, b_kv_ada, w_kv_b, w_o):
    bsz, seq, d = x.shape
    depth = w_ada.shape[0]
    n_a = w_qkv_a.shape[0]
    alpha = (2.0 * depth) ** 0.25
    t = bsz * seq
    assert seq % MOBA_BLOCK == 0 and seq % SB_BLOCK == 0 and d % (N_HEADS * LANES) == 0

    mod_all = _ada(c, w_ada, b_ada).reshape(depth, bsz, 3 * N_SUB, d)
    kv_mod = _ada(c, w_kv_ada[None], b_kv_ada[None]).reshape(1, bsz, 2, d)

    w_in = w_ffn_in.astype(BF16)
    w_out = w_ffn_out.astype(BF16)
    w_qkv = w_qkv_a.astype(BF16)
    w_q = w_q_b.astype(BF16)
    w_kv = w_kv_b.astype(BF16)[None]
    w_ob = w_o.astype(BF16)
    tabs = _rotary_tables(seq, d // N_HEADS)

    xf = x.reshape(t, d)
    ffn = functools.partial(_ffn, mod_all=mod_all, w_in=w_in, w_out=w_out, ln_g=ln_g, ln_b=ln_b, seq=seq, alpha=alpha)
    kv_proj = (kv_mod, 0, (0, 1), w_kv, 0)
    kv = None
    if n_a == 0 and depth > 0:
        kv = _proj(xf, kv_mod, w_kv, mod_layer=0, w_layer=0, r_shift=0, r_scale=1, seq=seq, name="kv_proj")
    for l in range(depth):
        if l < n_a:
            xf = ffn(xf, layer=l, which=0)
            q, k, v, kbar = _qkv(xf, mod_all, w_qkv, tabs, layer=l, seq=seq)
            att = _moba(q, k, v, kbar.reshape(bsz, seq // MOBA_BLOCK, d), batch=bsz, seq=seq)
        else:
            xf, q = ffn(xf, layer=l, which=0, proj=(mod_all, l, (3, 4), w_q, l - n_a))
            att = _sb(q, kv, batch=bsz, seq=seq)
        if l + 1 == n_a and l + 1 < depth:
            xf, kv = ffn(xf, layer=l, which=1, mixer=(att, w_ob), proj=kv_proj)
        else:
            xf = ffn(xf, layer=l, which=1, mixer=(att, w_ob))
    return xf.reshape(bsz, seq, d)
```

```python
import functools

import jax
import jax.numpy as jnp
from jax import lax
from jax.experimental import pallas as pl
from jax.experimental.pallas import tpu as pltpu

N_HEADS = 8
ROT_FRACTION = 4
ROPE_THETA = 500000.0
MOBA_BLOCK = 256
MOBA_TOPK = 3
SB_BLOCK = 256
SB_NEAR = 2
SB_DEAD_LOG2 = 160.0
LN_EPS = 1e-5
NEG_INF = -1e30
N_SUB = 3
LANES = 128
BF16_SUBLANES = 16
LOG2E = 1.4426950408889634

F32 = jnp.float32
BF16 = jnp.bfloat16

VMEM_LIMIT = 56 * 1024 * 1024
FFN_TILE = 1024


def _params(n_axes):
    return pltpu.CompilerParams(
        dimension_semantics=("parallel",) * n_axes, vmem_limit_bytes=VMEM_LIMIT)


def _dot(a, b):
    return jnp.dot(a, b, preferred_element_type=F32)


def _dot_t(a, b):
    return lax.dot_general(a, b, (((1,), (1,)), ((), ())), preferred_element_type=F32)


def _dot_tn(a, b):
    return lax.dot_general(a, b, (((0,), (0,)), ((), ())), preferred_element_type=F32)


def _layer_norm(r, g, b):
    mu = jnp.mean(r, axis=-1, keepdims=True)
    d = r - mu
    var = jnp.mean(d * d, axis=-1, keepdims=True)
    return d * lax.rsqrt(var + LN_EPS) * g + b


def _ada_kernel(c_ref, w_ref, b_ref, o_ref):
    c = c_ref[...]
    cond = (c * jax.nn.sigmoid(c)).astype(BF16)
    o_ref[...] = _dot(cond, w_ref[...].astype(BF16)) + b_ref[...]


def _ada(c, w, b, tn=1024):
    nl, d, n = w.shape
    bsz = c.shape[0]
    return pl.pallas_call(
        _ada_kernel,
        grid=(nl, n // tn),
        in_specs=[
            pl.BlockSpec((bsz, d), lambda l, j: (0, 0)),
            pl.BlockSpec((None, d, tn), lambda l, j: (l, 0, j)),
            pl.BlockSpec((None, 1, tn), lambda l, j: (l, 0, j)),
        ],
        out_specs=pl.BlockSpec((None, bsz, tn), lambda l, j: (l, 0, j)),
        out_shape=jax.ShapeDtypeStruct((nl, bsz, n), F32),
        compiler_params=_params(2),
        name="ada",
    )(c, w, b.reshape(nl, 1, n))


def _ffn_kernel(*refs, k0, sub, alpha, rows, mixer_out, proj_rows):
    it = iter(refs)
    x_ref, mod_ref = next(it), next(it)
    if mixer_out:
        att_ref, wmix_ref = next(it), next(it)
    wg_ref, wu_ref, wo_ref, lg_ref, lb_ref = (next(it) for _ in range(5))
    if proj_rows:
        pmod_ref, wproj_ref = next(it), next(it)
    o_ref = next(it)
    if proj_rows:
        p_ref = next(it)

    shift = mod_ref[0, k0:k0 + 1, :]
    scale = mod_ref[0, k0 + 1:k0 + 2, :]
    gate = mod_ref[0, k0 + 2:k0 + 3, :]
    def stream_in(r0):
        x = x_ref[r0:r0 + rows, :]
        if mixer_out:
            y = _dot(att_ref[r0:r0 + rows, :], wmix_ref[...])
            x = _layer_norm(alpha * x + (1.0 + mod_ref[0, 5:6, :]) * y, lg_ref[1:2, :], lb_ref[1:2, :])
        return x

    def project(r0, x):
        r_shift, r_scale = proj_rows
        hp = (x * (1.0 + pmod_ref[0, r_scale:r_scale + 1, :]) + pmod_ref[0, r_shift:r_shift + 1, :]).astype(BF16)
        p_ref[r0:r0 + rows, :] = _dot(hp, wproj_ref[...]).astype(p_ref.dtype)

    starts = list(range(0, x_ref.shape[0], rows))
    x_next = stream_in(starts[0])
    project_prev = None
    for n, r0 in enumerate(starts):
        x = x_next
        if n + 1 < len(starts):
            x_next = stream_in(starts[n + 1])
        h = (x * (1.0 + scale) + shift).astype(BF16)
        g = _dot(h, wg_ref[...])
        u = _dot(h, wu_ref[...])
        if project_prev is not None:
            project_prev()
            project_prev = None
        a = (g * jax.nn.sigmoid(g) * u).astype(BF16)
        y = _dot(a, wo_ref[...])
        r = alpha * x + (0.5 * (1.0 + gate)) * y
        x = _layer_norm(r, lg_ref[sub:sub + 1, :], lb_ref[sub:sub + 1, :])
        o_ref[r0:r0 + rows, :] = x
        if proj_rows:
            project_prev = functools.partial(project, r0, x)
    if project_prev is not None:
        project_prev()


def _ffn_tile(d, f, seq, rows, *, with_mixer, n_proj):
    weights = 2 * (3 * d * f + (d * d if with_mixer else 0) + d * n_proj)
    per_token = 2 * (4 * d + 4 * d + (2 * d if with_mixer else 0) + 2 * n_proj)
    temporaries = rows * (2 * f * 4 + f * 2 + 4 * d * 4)
    tm = FFN_TILE
    while tm > rows and (seq % tm or weights + per_token * tm + temporaries > VMEM_LIMIT):
        tm //= 2
    return tm


def _ffn(x, mod_all, w_in, w_out, ln_g, ln_b, *, layer, which, seq, alpha, mixer=None, proj=None, rows=256):
    t, d = x.shape
    f = w_out.shape[2]
    k0 = 0 if which == 0 else 6
    sub = 0 if which == 0 else 2
    n_proj = 0 if proj is None else proj[3].shape[2]
    tm = _ffn_tile(d, f, seq, rows, with_mixer=mixer is not None, n_proj=n_proj)
    per_seq = seq // tm
    tok_spec = pl.BlockSpec((tm, d), lambda i: (i, 0))
    ln_spec = pl.BlockSpec((None, N_SUB, d), lambda i: (layer, 0, 0))
    args = [x, mod_all]
    in_specs = [tok_spec, pl.BlockSpec((None, 1, 3 * N_SUB, d), lambda i: (layer, i // per_seq, 0, 0))]
    if mixer is not None:
        att, w_mix = mixer
        args += [att, w_mix]
        in_specs += [tok_spec, pl.BlockSpec((None, d, d), lambda i: (layer, 0, 0))]
    args += [w_in, w_in, w_out, ln_g, ln_b]
    in_specs += [
        pl.BlockSpec((None, None, d, f), lambda i: (layer, which, 0, 0)),
        pl.BlockSpec((None, None, d, f), lambda i: (layer, which, 0, 1)),
        pl.BlockSpec((None, None, f, d), lambda i: (layer, which, 0, 0)),
        ln_spec, ln_spec,
    ]
    out_specs = [tok_spec]
    out_shape = [jax.ShapeDtypeStruct((t, d), F32)]
    proj_rows = None
    if proj is not None:
        pmod, mod_layer, proj_rows, w_proj, w_layer = proj
        n = w_proj.shape[2]
        args += [pmod, w_proj]
        in_specs += [pl.BlockSpec((None, 1, pmod.shape[2], d), lambda i: (mod_layer, i // per_seq, 0, 0)),
                     pl.BlockSpec((None, d, n), lambda i: (w_layer, 0, 0))]
        out_specs.append(pl.BlockSpec((tm, n), lambda i: (i, 0)))
        out_shape.append(jax.ShapeDtypeStruct((t, n), BF16))
    kern = functools.partial(_ffn_kernel, k0=k0, sub=sub, alpha=alpha, rows=rows,
                             mixer_out=mixer is not None, proj_rows=proj_rows)
    outs = pl.pallas_call(
        kern,
        grid=(t // tm,),
        in_specs=in_specs,
        out_specs=out_specs,
        out_shape=out_shape,
        compiler_params=_params(1),
        name=f"ffn_l{layer}_{which}",
    )(*args)
    return outs if proj is not None else outs[0]


def _rotary_tables(seq, head_dim):
    rot = head_dim // ROT_FRACTION
    half = rot // 2
    inv_freq = jnp.power(ROPE_THETA, -jnp.arange(half, dtype=F32) * 2.0 / rot)
    ang = jnp.arange(seq, dtype=F32)[:, None] * inv_freq[None, :]
    cos, sin = jnp.cos(ang), jnp.sin(ang)
    ones = jnp.ones((seq, head_dim - rot), F32)
    zeros = jnp.zeros((seq, head_dim - rot), F32)
    zh = jnp.zeros((seq, half), F32)
    c_tab = jnp.concatenate([cos, cos, ones], axis=1)
    up_tab = jnp.concatenate([zh, sin, zeros], axis=1)
    dn_tab = jnp.concatenate([-sin, zh, zeros], axis=1)
    return c_tab, up_tab, dn_tab


def _qkv_kernel(x_ref, mod_ref, w_ref, c_ref, up_ref, dn_ref, q_ref, k_ref, v_ref, kbar_ref, *, half):
    x = x_ref[...]
    d = x.shape[1]
    hd = c_ref.shape[1]
    shift = mod_ref[0, 3:4, :]
    scale = mod_ref[0, 4:5, :]
    h = (x * (1.0 + scale) + shift).astype(BF16)
    qkv = _dot(h, w_ref[...])
    c_tab, up_tab, dn_tab = c_ref[...], up_ref[...], dn_ref[...]

    def rotate(t):
        parts = []
        for hh in range(d // hd):
            th = t[:, hh * hd:(hh + 1) * hd]
            parts.append(th * c_tab
                         + pltpu.roll(th, half, 1) * up_tab
                         + pltpu.roll(th, hd - half, 1) * dn_tab)
        return jnp.concatenate(parts, axis=1)

    q = rotate(qkv[:, :d])
    k = rotate(qkv[:, d:2 * d])
    q_ref[...] = q.astype(q_ref.dtype)
    k_ref[...] = k.astype(k_ref.dtype)
    v_ref[...] = qkv[:, 2 * d:].astype(v_ref.dtype)
    for r in range(k.shape[0] // MOBA_BLOCK):
        kbar_ref[0, r:r + 1, :] = jnp.mean(k[r * MOBA_BLOCK:(r + 1) * MOBA_BLOCK], axis=0, keepdims=True)


def _qkv(x, mod_all, w, tabs, *, layer, seq, tm=512):
    t, d = x.shape
    hd = d // N_HEADS
    per_seq = seq // tm
    nb = tm // MOBA_BLOCK
    kern = functools.partial(_qkv_kernel, half=hd // ROT_FRACTION // 2)
    tab_spec = pl.BlockSpec((tm, hd), lambda i: (i % per_seq, 0))
    tok_spec = pl.BlockSpec((tm, d), lambda i: (i, 0))
    return pl.pallas_call(
        kern,
        grid=(t // tm,),
        in_specs=[
            tok_spec,
            pl.BlockSpec((None, 1, 3 * N_SUB, d), lambda i: (layer, i // per_seq, 0, 0)),
            pl.BlockSpec((None, d, 3 * d), lambda i: (layer, 0, 0)),
            tab_spec, tab_spec, tab_spec,
        ],
        out_specs=[tok_spec, tok_spec, tok_spec,
                   pl.BlockSpec((1, nb, d), lambda i: (i, 0, 0))],
        out_shape=[jax.ShapeDtypeStruct((t, d), BF16)] * 3
        + [jax.ShapeDtypeStruct((t // tm, nb, d), F32)],
        compiler_params=_params(1),
        name=f"qkv_l{layer}",
    )(x, mod_all, w, *tabs)


def _proj_kernel(x_ref, mod_ref, w_ref, o_ref, *, r_shift, r_scale):
    x = x_ref[...]
    shift = mod_ref[0, r_shift:r_shift + 1, :]
    scale = mod_ref[0, r_scale:r_scale + 1, :]
    h = (x * (1.0 + scale) + shift).astype(BF16)
    o_ref[...] = _dot(h, w_ref[...]).astype(o_ref.dtype)


def _proj(x, mod, w, *, mod_layer, w_layer, r_shift, r_scale, seq, tm=512, name="proj"):
    t, d = x.shape
    n = w.shape[2]
    rows = mod.shape[2]
    per_seq = seq // tm
    kern = functools.partial(_proj_kernel, r_shift=r_shift, r_scale=r_scale)
    return pl.pallas_call(
        kern,
        grid=(t // tm,),
        in_specs=[
            pl.BlockSpec((tm, d), lambda i: (i, 0)),
            pl.BlockSpec((None, 1, rows, d), lambda i: (mod_layer, i // per_seq, 0, 0)),
            pl.BlockSpec((None, d, n), lambda i: (w_layer, 0, 0)),
        ],
        out_specs=pl.BlockSpec((tm, n), lambda i: (i, 0)),
        out_shape=jax.ShapeDtypeStruct((t, n), BF16),
        compiler_params=_params(1),
        name=name,
    )(x, mod, w)


def _moba_kernel(q_ref, k_ref, v_ref, kbar_ref, o_ref, vt_ref, s_ref, p_ref, *, nblk, scale):
    blk = MOBA_BLOCK
    hd = q_ref.shape[1]
    grows = -(-nblk // BF16_SUBLANES) * BF16_SUBLANES
    kb = jnp.concatenate([kbar_ref[0], jnp.zeros((grows - nblk, hd), F32)], axis=0)
    kb_hi = kb.astype(BF16)
    kb_lo = (kb - kb_hi.astype(F32)).astype(BF16)
    key = lax.broadcasted_iota(jnp.int32, (blk, blk), 0)
    qry = lax.broadcasted_iota(jnp.int32, (blk, blk), 1)
    causal = key <= qry
    grow = lax.broadcasted_iota(jnp.int32, (grows, blk), 0).astype(F32)
    c2 = scale * LOG2E

    for j in range(nblk):
        vt_ref[0:hd, j * blk:(j + 1) * blk] = v_ref[j * blk:(j + 1) * blk, :].astype(F32).T.astype(BF16)
    vt_ref[hd:, :] = jnp.ones((vt_ref.shape[0] - hd, vt_ref.shape[1]), BF16)

    def select_blocks(i):
        if i <= MOBA_TOPK:
            return None
        qi = q_ref[i * blk:(i + 1) * blk, :]
        gate = _dot_t(kb_hi, qi) + _dot_t(kb_lo, qi)
        gate = jnp.where(grow < float(i), gate, -jnp.inf)
        sel = jnp.full((grows, blk), NEG_INF, F32)
        for _ in range(MOBA_TOPK):
            top = jnp.max(gate, axis=0, keepdims=True)
            first = jnp.min(jnp.where(gate == top, grow, float(grows)), axis=0, keepdims=True)
            hit = grow == first
            sel = jnp.where(hit, 0.0, sel)
            gate = jnp.where(hit, -jnp.inf, gate)
        return sel

    def scores(i, j, sel, m):
        s = _dot_t(k_ref[j * blk:(j + 1) * blk, :], q_ref[i * blk:(i + 1) * blk, :]) * c2
        s_ref[i % 2, j * blk:(j + 1) * blk, :] = s
        if j == i:
            s = jnp.where(causal, s, NEG_INF)
        elif sel is not None:
            s = s + sel[j:j + 1, :]
        top = jnp.max(s, axis=0, keepdims=True)
        return top if m is None else jnp.maximum(m, top)

    def weighted_values(i):
        acc = _dot(vt_ref[:, 0:(i + 1) * blk], p_ref[i % 2, 0:(i + 1) * blk, :])
        o_ref[i * blk:(i + 1) * blk, :] = (acc[:hd, :] * (1.0 / acc[hd:hd + 1, :])).T.astype(o_ref.dtype)

    sel, m = None, scores(0, 0, None, None)
    for i in range(nblk):
        if i + 1 < nblk:
            sel_next = select_blocks(i + 1)
        if i > 0:
            weighted_values(i - 1)
        m_next = None
        for j in range(i + 1):
            if i + 1 < nblk:
                m_next = scores(i + 1, j, sel_next, m_next)
            s = s_ref[i % 2, j * blk:(j + 1) * blk, :]
            if j == i:
                p = jnp.exp2(jnp.where(causal, s, NEG_INF) - m)
            else:
                p = jnp.exp2(s + ((-m) if sel is None else (sel[j:j + 1, :] - m)))
            p_ref[i % 2, j * blk:(j + 1) * blk, :] = p.astype(BF16)
        if i + 1 < nblk:
            m_next = scores(i + 1, i + 1, sel_next, m_next)
            sel, m = sel_next, m_next
    weighted_values(nblk - 1)


def _moba(q, k, v, kbar, *, batch, seq):
    t, d = q.shape
    hd = d // N_HEADS
    nblk = seq // MOBA_BLOCK
    kern = functools.partial(_moba_kernel, nblk=nblk, scale=hd ** -0.5)
    head_spec = pl.BlockSpec((seq, hd), lambda b, h: (b, h))
    return pl.pallas_call(
        kern,
        grid=(batch, N_HEADS),
        in_specs=[head_spec, head_spec, head_spec,
                  pl.BlockSpec((1, nblk, hd), lambda b, h: (b, 0, h))],
        out_specs=head_spec,
        out_shape=jax.ShapeDtypeStruct((t, d), BF16),
        scratch_shapes=[pltpu.VMEM((hd + BF16_SUBLANES, seq), BF16),
                        pltpu.VMEM((2, seq, MOBA_BLOCK), F32),
                        pltpu.VMEM((2, seq, MOBA_BLOCK), BF16)],
        compiler_params=_params(2),
        name="moba",
    )(q, k, v, kbar)


def _softplus2(z2):
    neg_abs = lax.bitcast_convert_type(
        lax.bitcast_convert_type(z2, jnp.uint32) | jnp.uint32(0x80000000), F32)
    return jnp.maximum(z2, 0.0) + jnp.log2(1.0 + jnp.exp2(neg_abs))


def _sb_kernel(q_ref, k_ref, v_ref, o_ref, vt_ref, z_ref, w_ref, acc_ref, tail_ref, *, nblk, scale):
    blk = SB_BLOCK
    key = lax.broadcasted_iota(jnp.int32, (blk, blk), 0)
    qry = lax.broadcasted_iota(jnp.int32, (blk, blk), 1)
    strict = key < qry
    later = (qry > key).astype(BF16)
    later2 = jnp.concatenate([later, later], axis=1)
    c2 = scale * LOG2E

    def sum_later(sp):
        hi = sp.astype(BF16)
        lo = (sp - hi.astype(F32)).astype(BF16)
        return _dot(later2, jnp.concatenate([hi, lo], axis=0))

    for j in range(nblk):
        vt_ref[:, j * blk:(j + 1) * blk] = v_ref[j * blk:(j + 1) * blk, :].astype(F32).T.astype(BF16)

    def logits(i, b0, b1):
        z_ref[i % 2, b0 * blk:b1 * blk, :] = _dot_t(k_ref[b0 * blk:b1 * blk, :], q_ref[i * blk:(i + 1) * blk, :]) * c2

    def block_weights(i, j, tail):
        z = z_ref[i % 2, j * blk:(j + 1) * blk, :]
        sp = _softplus2(z)
        if j == i:
            sp = jnp.where(strict, sp, 0.0)
        e = z - sp - sum_later(sp)
        w = jnp.exp2(e if tail is None else e - tail)
        if j == i:
            w = jnp.where(strict, w, 0.0)
        w_ref[i % 2, j * blk:(j + 1) * blk, :] = w.astype(BF16)
        if j == 0:
            return tail
        part = jnp.sum(sp, axis=0, keepdims=True)
        return part if tail is None else tail + part

    def near_values(i):
        b0 = max(i + 1 - SB_NEAR, 0)
        acc = _dot(vt_ref[:, b0 * blk:(i + 1) * blk], w_ref[i % 2, b0 * blk:(i + 1) * blk, :])
        if b0 > 0:
            acc_ref[i] = acc
        o_ref[i * blk:(i + 1) * blk, :] = acc.T.astype(o_ref.dtype)

    logits(0, 0, 1)
    least_tail = None
    for i in range(nblk):
        if i + 1 < nblk:
            logits(i + 1, max(i + 2 - SB_NEAR, 0), i + 2)
        if i > 0:
            near_values(i - 1)
        b0 = max(i + 1 - SB_NEAR, 0)
        tail = None
        for j in range(i, b0 - 1, -1):
            tail = block_weights(i, j, tail)
        if b0 > 0:
            tail_ref[i:i + 1, :] = tail
            least_tail = tail if least_tail is None else jnp.minimum(least_tail, tail)
    near_values(nblk - 1)

    if nblk > SB_NEAR:
        @pl.when(jnp.min(least_tail) < SB_DEAD_LOG2)
        def _():
            for i in range(SB_NEAR, nblk):
                b1 = i + 1 - SB_NEAR
                logits(i, 0, b1)
                tail = tail_ref[i:i + 1, :]
                for j in range(b1 - 1, -1, -1):
                    tail = block_weights(i, j, tail)
                acc = acc_ref[i] + _dot(vt_ref[:, 0:b1 * blk], w_ref[i % 2, 0:b1 * blk, :])
                o_ref[i * blk:(i + 1) * blk, :] = acc.T.astype(o_ref.dtype)


def _sb(q, kv, *, batch, seq):
    t, d = q.shape
    hd = d // N_HEADS
    nblk = seq // SB_BLOCK
    kern = functools.partial(_sb_kernel, nblk=nblk, scale=hd ** -0.5)
    return pl.pallas_call(
        kern,
        grid=(batch, N_HEADS),
        in_specs=[pl.BlockSpec((seq, hd), lambda b, h: (b, h)),
                  pl.BlockSpec((seq, hd), lambda b, h: (b, h)),
                  pl.BlockSpec((seq, hd), lambda b, h: (b, N_HEADS + h))],
        out_specs=pl.BlockSpec((seq, hd), lambda b, h: (b, h)),
        out_shape=jax.ShapeDtypeStruct((t, d), BF16),
        scratch_shapes=[pltpu.VMEM((hd, seq), BF16),
                        pltpu.VMEM((2, seq, SB_BLOCK), F32),
                        pltpu.VMEM((2, seq, SB_BLOCK), BF16),
                        pltpu.VMEM((nblk, hd, SB_BLOCK), F32),
                        pltpu.VMEM((nblk, SB_BLOCK), F32)],
        compiler_params=_params(2),
        name="stickbreak",
    )(q, kv, kv)


def kernel(x, c, w_ada, b_ada, ln_g, ln_b, w_ffn_in, w_ffn_out, w_qkv_a, w_q_b, w_kv_ada, b_kv_ada, w_kv_b, w_o):
    bsz, seq, d = x.shape
    depth = w_ada.shape[0]
    n_a = w_qkv_a.shape[0]
    alpha = (2.0 * depth) ** 0.25
    t = bsz * seq
    assert seq % MOBA_BLOCK == 0 and seq % SB_BLOCK == 0 and d % (N_HEADS * LANES) == 0

    mod_all = _ada(c, w_ada, b_ada).reshape(depth, bsz, 3 * N_SUB, d)
    kv_mod = _ada(c, w_kv_ada[None], b_kv_ada[None]).reshape(1, bsz, 2, d)

    w_in = w_ffn_in.astype(BF16)
    w_out = w_ffn_out.astype(BF16)
    w_qkv = w_qkv_a.astype(BF16)
    w_q = w_q_b.astype(BF16)
    w_kv = w_kv_b.astype(BF16)[None]
    w_ob = w_o.astype(BF16)
    tabs = _rotary_tables(seq, d // N_HEADS)

    xf = x.reshape(t, d)
    ffn = functools.partial(_ffn, mod_all=mod_all, w_in=w_in, w_out=w_out, ln_g=ln_g, ln_b=ln_b, seq=seq, alpha=alpha)
    kv_proj = (kv_mod, 0, (0, 1), w_kv, 0)
    kv = None
    if n_a == 0 and depth > 0:
        kv = _proj(xf, kv_mod, w_kv, mod_layer=0, w_layer=0, r_shift=0, r_scale=1, seq=seq, name="kv_proj")
    for l in range(depth):
        if l < n_a:
            xf = ffn(xf, layer=l, which=0)
            q, k, v, kbar = _qkv(xf, mod_all, w_qkv, tabs, layer=l, seq=seq)
            att = _moba(q, k, v, kbar.reshape(bsz, seq // MOBA_BLOCK, d), batch=bsz, seq=seq)
        else:
            xf, q = ffn(xf, layer=l, which=0, proj=(mod_all, l, (3, 4), w_q, l - n_a))
            att = _sb(q, kv, batch=bsz, seq=seq)
        if l + 1 == n_a and l + 1 < depth:
            xf, kv = ffn(xf, layer=l, which=1, mixer=(att, w_ob), proj=kv_proj)
        else:
            xf = ffn(xf, layer=l, which=1, mixer=(att, w_ob))
    return xf.reshape(bsz, seq, d)
```

```python
import functools

import jax
import jax.numpy as jnp
from jax import lax
from jax.experimental import pallas as pl
from jax.experimental.pallas import tpu as pltpu

N_HEADS = 8
ROT_FRACTION = 4
ROPE_THETA = 500000.0
MOBA_BLOCK = 256
MOBA_TOPK = 3
SB_BLOCK = 256
SB_NEAR = 2
SB_DEAD_LOG2 = 160.0
LN_EPS = 1e-5
NEG_INF = -1e30
N_SUB = 3
LANES = 128
BF16_SUBLANES = 16
LOG2E = 1.4426950408889634

F32 = jnp.float32
BF16 = jnp.bfloat16

VMEM_LIMIT = 60 * 1024 * 1024
FFN_TILE = 1024


def _params(n_axes):
    return pltpu.CompilerParams(
        dimension_semantics=("parallel",) * n_axes, vmem_limit_bytes=VMEM_LIMIT)


def _dot(a, b):
    return jnp.dot(a, b, preferred_element_type=F32)


def _dot_t(a, b):
    return lax.dot_general(a, b, (((1,), (1,)), ((), ())), preferred_element_type=F32)


def _dot_tn(a, b):
    return lax.dot_general(a, b, (((0,), (0,)), ((), ())), preferred_element_type=F32)


def _layer_norm(r, g, b):
    mu = jnp.mean(r, axis=-1, keepdims=True)
    d = r - mu
    var = jnp.mean(d * d, axis=-1, keepdims=True)
    return d * lax.rsqrt(var + LN_EPS) * g + b


def _ada_kernel(c_ref, w_ref, b_ref, o_ref):
    c = c_ref[...]
    cond = (c * jax.nn.sigmoid(c)).astype(BF16)
    o_ref[...] = _dot(cond, w_ref[...].astype(BF16)) + b_ref[...]


def _ada(c, w, b, tn=1024):
    nl, d, n = w.shape
    bsz = c.shape[0]
    return pl.pallas_call(
        _ada_kernel,
        grid=(nl, n // tn),
        in_specs=[
            pl.BlockSpec((bsz, d), lambda l, j: (0, 0)),
            pl.BlockSpec((None, d, tn), lambda l, j: (l, 0, j)),
            pl.BlockSpec((None, 1, tn), lambda l, j: (l, 0, j)),
        ],
        out_specs=pl.BlockSpec((None, bsz, tn), lambda l, j: (l, 0, j)),
        out_shape=jax.ShapeDtypeStruct((nl, bsz, n), F32),
        compiler_params=_params(2),
        name="ada",
    )(c, w, b.reshape(nl, 1, n))


def _ffn_kernel(*refs, k0, sub, alpha, rows, mixer_out, proj_rows):
    it = iter(refs)
    x_ref, mod_ref = next(it), next(it)
    if mixer_out:
        att_ref, wmix_ref = next(it), next(it)
    wg_ref, wu_ref, wo_ref, lg_ref, lb_ref = (next(it) for _ in range(5))
    if proj_rows:
        pmod_ref, wproj_ref = next(it), next(it)
    o_ref = next(it)
    if proj_rows:
        p_ref = next(it)

    shift = mod_ref[0, k0:k0 + 1, :]
    scale = mod_ref[0, k0 + 1:k0 + 2, :]
    gate = mod_ref[0, k0 + 2:k0 + 3, :]
    def stream_in(r0):
        x = x_ref[r0:r0 + rows, :]
        if mixer_out:
            y = _dot(att_ref[r0:r0 + rows, :], wmix_ref[...])
            x = _layer_norm(alpha * x + (1.0 + mod_ref[0, 5:6, :]) * y, lg_ref[1:2, :], lb_ref[1:2, :])
        return x

    def project(r0, x):
        r_shift, r_scale = proj_rows
        hp = (x * (1.0 + pmod_ref[0, r_scale:r_scale + 1, :]) + pmod_ref[0, r_shift:r_shift + 1, :]).astype(BF16)
        p_ref[r0:r0 + rows, :] = _dot(hp, wproj_ref[...]).astype(p_ref.dtype)

    starts = list(range(0, x_ref.shape[0], rows))
    x_next = stream_in(starts[0])
    project_prev = None
    for n, r0 in enumerate(starts):
        x = x_next
        if n + 1 < len(starts):
            x_next = stream_in(starts[n + 1])
        h = (x * (1.0 + scale) + shift).astype(BF16)
        g = _dot(h, wg_ref[...])
        u = _dot(h, wu_ref[...])
        if project_prev is not None:
            project_prev()
            project_prev = None
        a = (g * jax.nn.sigmoid(g) * u).astype(BF16)
        y = _dot(a, wo_ref[...])
        r = alpha * x + (0.5 * (1.0 + gate)) * y
        x = _layer_norm(r, lg_ref[sub:sub + 1, :], lb_ref[sub:sub + 1, :])
        o_ref[r0:r0 + rows, :] = x
        if proj_rows:
            project_prev = functools.partial(project, r0, x)
    if project_prev is not None:
        project_prev()


def _ffn_tile(d, f, seq, rows, *, with_mixer, n_proj):
    weights = 2 * (3 * d * f + (d * d if with_mixer else 0) + d * n_proj)
    per_token = 2 * (4 * d + 4 * d + (2 * d if with_mixer else 0) + 2 * n_proj)
    temporaries = rows * (2 * f * 4 + f * 2)
    tm = FFN_TILE
    while tm > rows and (seq % tm or weights + per_token * tm + temporaries > VMEM_LIMIT):
        tm //= 2
    return tm


def _ffn(x, mod_all, w_in, w_out, ln_g, ln_b, *, layer, which, seq, alpha, mixer=None, proj=None, rows=256):
    t, d = x.shape
    f = w_out.shape[2]
    k0 = 0 if which == 0 else 6
    sub = 0 if which == 0 else 2
    n_proj = 0 if proj is None else proj[3].shape[2]
    tm = _ffn_tile(d, f, seq, rows, with_mixer=mixer is not None, n_proj=n_proj)
    per_seq = seq // tm
    tok_spec = pl.BlockSpec((tm, d), lambda i: (i, 0))
    ln_spec = pl.BlockSpec((None, N_SUB, d), lambda i: (layer, 0, 0))
    args = [x, mod_all]
    in_specs = [tok_spec, pl.BlockSpec((None, 1, 3 * N_SUB, d), lambda i: (layer, i // per_seq, 0, 0))]
    if mixer is not None:
        att, w_mix = mixer
        args += [att, w_mix]
        in_specs += [tok_spec, pl.BlockSpec((None, d, d), lambda i: (layer, 0, 0))]
    args += [w_in, w_in, w_out, ln_g, ln_b]
    in_specs += [
        pl.BlockSpec((None, None, d, f), lambda i: (layer, which, 0, 0)),
        pl.BlockSpec((None, None, d, f), lambda i: (layer, which, 0, 1)),
        pl.BlockSpec((None, None, f, d), lambda i: (layer, which, 0, 0)),
        ln_spec, ln_spec,
    ]
    out_specs = [tok_spec]
    out_shape = [jax.ShapeDtypeStruct((t, d), F32)]
    proj_rows = None
    if proj is not None:
        pmod, mod_layer, proj_rows, w_proj, w_layer = proj
        n = w_proj.shape[2]
        args += [pmod, w_proj]
        in_specs += [pl.BlockSpec((None, 1, pmod.shape[2], d), lambda i: (mod_layer, i // per_seq, 0, 0)),
                     pl.BlockSpec((None, d, n), lambda i: (w_layer, 0, 0))]
        out_specs.append(pl.BlockSpec((tm, n), lambda i: (i, 0)))
        out_shape.append(jax.ShapeDtypeStruct((t, n), BF16))
    kern = functools.partial(_ffn_kernel, k0=k0, sub=sub, alpha=alpha, rows=rows,
                             mixer_out=mixer is not None, proj_rows=proj_rows)
    outs = pl.pallas_call(
        kern,
        grid=(t // tm,),
        in_specs=in_specs,
        out_specs=out_specs,
        out_shape=out_shape,
        compiler_params=_params(1),
        name=f"ffn_l{layer}_{which}",
    )(*args)
    return outs if proj is not None else outs[0]


def _rotary_tables(seq, head_dim):
    rot = head_dim // ROT_FRACTION
    half = rot // 2
    inv_freq = jnp.power(ROPE_THETA, -jnp.arange(half, dtype=F32) * 2.0 / rot)
    ang = jnp.arange(seq, dtype=F32)[:, None] * inv_freq[None, :]
    cos, sin = jnp.cos(ang), jnp.sin(ang)
    ones = jnp.ones((seq, head_dim - rot), F32)
    zeros = jnp.zeros((seq, head_dim - rot), F32)
    zh = jnp.zeros((seq, half), F32)
    c_tab = jnp.concatenate([cos, cos, ones], axis=1)
    up_tab = jnp.concatenate([zh, sin, zeros], axis=1)
    dn_tab = jnp.concatenate([-sin, zh, zeros], axis=1)
    return c_tab, up_tab, dn_tab


def _qkv_kernel(x_ref, mod_ref, w_ref, c_ref, up_ref, dn_ref, q_ref, k_ref, v_ref, kbar_ref, *, half):
    x = x_ref[...]
    d = x.shape[1]
    hd = c_ref.shape[1]
    shift = mod_ref[0, 3:4, :]
    scale = mod_ref[0, 4:5, :]
    h = (x * (1.0 + scale) + shift).astype(BF16)
    qkv = _dot(h, w_ref[...])
    c_tab, up_tab, dn_tab = c_ref[...], up_ref[...], dn_ref[...]

    def rotate(t):
        parts = []
        for hh in range(d // hd):
            th = t[:, hh * hd:(hh + 1) * hd]
            parts.append(th * c_tab
                         + pltpu.roll(th, half, 1) * up_tab
                         + pltpu.roll(th, hd - half, 1) * dn_tab)
        return jnp.concatenate(parts, axis=1)

    q = rotate(qkv[:, :d])
    k = rotate(qkv[:, d:2 * d])
    q_ref[...] = q.astype(q_ref.dtype)
    k_ref[...] = k.astype(k_ref.dtype)
    v_ref[...] = qkv[:, 2 * d:].astype(v_ref.dtype)
    for r in range(k.shape[0] // MOBA_BLOCK):
        kbar_ref[0, r:r + 1, :] = jnp.mean(k[r * MOBA_BLOCK:(r + 1) * MOBA_BLOCK], axis=0, keepdims=True)


def _qkv(x, mod_all, w, tabs, *, layer, seq, tm=512):
    t, d = x.shape
    hd = d // N_HEADS
    per_seq = seq // tm
    nb = tm // MOBA_BLOCK
    kern = functools.partial(_qkv_kernel, half=hd // ROT_FRACTION // 2)
    tab_spec = pl.BlockSpec((tm, hd), lambda i: (i % per_seq, 0))
    tok_spec = pl.BlockSpec((tm, d), lambda i: (i, 0))
    return pl.pallas_call(
        kern,
        grid=(t // tm,),
        in_specs=[
            tok_spec,
            pl.BlockSpec((None, 1, 3 * N_SUB, d), lambda i: (layer, i // per_seq, 0, 0)),
            pl.BlockSpec((None, d, 3 * d), lambda i: (layer, 0, 0)),
            tab_spec, tab_spec, tab_spec,
        ],
        out_specs=[tok_spec, tok_spec, tok_spec,
                   pl.BlockSpec((1, nb, d), lambda i: (i, 0, 0))],
        out_shape=[jax.ShapeDtypeStruct((t, d), BF16)] * 3
        + [jax.ShapeDtypeStruct((t // tm, nb, d), F32)],
        compiler_params=_params(1),
        name=f"qkv_l{layer}",
    )(x, mod_all, w, *tabs)


def _proj_kernel(x_ref, mod_ref, w_ref, o_ref, *, r_shift, r_scale):
    x = x_ref[...]
    shift = mod_ref[0, r_shift:r_shift + 1, :]
    scale = mod_ref[0, r_scale:r_scale + 1, :]
    h = (x * (1.0 + scale) + shift).astype(BF16)
    o_ref[...] = _dot(h, w_ref[...]).astype(o_ref.dtype)


def _proj(x, mod, w, *, mod_layer, w_layer, r_shift, r_scale, seq, tm=512, name="proj"):
    t, d = x.shape
    n = w.shape[2]
    rows = mod.shape[2]
    per_seq = seq // tm
    kern = functools.partial(_proj_kernel, r_shift=r_shift, r_scale=r_scale)
    return pl.pallas_call(
        kern,
        grid=(t // tm,),
        in_specs=[
            pl.BlockSpec((tm, d), lambda i: (i, 0)),
            pl.BlockSpec((None, 1, rows, d), lambda i: (mod_layer, i // per_seq, 0, 0)),
            pl.BlockSpec((None, d, n), lambda i: (w_layer, 0, 0)),
        ],
        out_specs=pl.BlockSpec((tm, n), lambda i: (i, 0)),
        out_shape=jax.ShapeDtypeStruct((t, n), BF16),
        compiler_params=_params(1),
        name=name,
    )(x, mod, w)


def _moba_kernel(q_ref, k_ref, v_ref, kbar_ref, o_ref, vt_ref, s_ref, p_ref, *, nblk, scale):
    blk = MOBA_BLOCK
    hd = q_ref.shape[1]
    grows = -(-nblk // BF16_SUBLANES) * BF16_SUBLANES
    kb = jnp.concatenate([kbar_ref[0], jnp.zeros((grows - nblk, hd), F32)], axis=0)
    kb_hi = kb.astype(BF16)
    kb_lo = (kb - kb_hi.astype(F32)).astype(BF16)
    key = lax.broadcasted_iota(jnp.int32, (blk, blk), 0)
    qry = lax.broadcasted_iota(jnp.int32, (blk, blk), 1)
    causal = key <= qry
    grow = lax.broadcasted_iota(jnp.int32, (grows, blk), 0).astype(F32)
    c2 = scale * LOG2E

    for j in range(nblk):
        vt_ref[0:hd, j * blk:(j + 1) * blk] = v_ref[j * blk:(j + 1) * blk, :].astype(F32).T.astype(BF16)
    vt_ref[hd:, :] = jnp.ones((vt_ref.shape[0] - hd, vt_ref.shape[1]), BF16)

    def select_blocks(i):
        if i <= MOBA_TOPK:
            return None
        qi = q_ref[i * blk:(i + 1) * blk, :]
        gate = _dot_t(kb_hi, qi) + _dot_t(kb_lo, qi)
        gate = jnp.where(grow < float(i), gate, -jnp.inf)
        sel = jnp.full((grows, blk), NEG_INF, F32)
        for _ in range(MOBA_TOPK):
            top = jnp.max(gate, axis=0, keepdims=True)
            first = jnp.min(jnp.where(gate == top, grow, float(grows)), axis=0, keepdims=True)
            hit = grow == first
            sel = jnp.where(hit, 0.0, sel)
            gate = jnp.where(hit, -jnp.inf, gate)
        return sel

    def scores(i, j, sel, m):
        s = _dot_t(k_ref[j * blk:(j + 1) * blk, :], q_ref[i * blk:(i + 1) * blk, :]) * c2
        s_ref[i % 2, j * blk:(j + 1) * blk, :] = s
        if j == i:
            s = jnp.where(causal, s, NEG_INF)
        elif sel is not None:
            s = s + sel[j:j + 1, :]
        top = jnp.max(s, axis=0, keepdims=True)
        return top if m is None else jnp.maximum(m, top)

    def weighted_values(i):
        acc = _dot(vt_ref[:, 0:(i + 1) * blk], p_ref[i % 2, 0:(i + 1) * blk, :])
        o_ref[i * blk:(i + 1) * blk, :] = (acc[:hd, :] * (1.0 / acc[hd:hd + 1, :])).T.astype(o_ref.dtype)

    sel, m = None, scores(0, 0, None, None)
    for i in range(nblk):
        if i + 1 < nblk:
            sel_next = select_blocks(i + 1)
        if i > 0:
            weighted_values(i - 1)
        m_next = None
        for j in range(i + 1):
            if i + 1 < nblk:
                m_next = scores(i + 1, j, sel_next, m_next)
            s = s_ref[i % 2, j * blk:(j + 1) * blk, :]
            if j == i:
                p = jnp.exp2(jnp.where(causal, s, NEG_INF) - m)
            else:
                p = jnp.exp2(s + ((-m) if sel is None else (sel[j:j + 1, :] - m)))
            p_ref[i % 2, j * blk:(j + 1) * blk, :] = p.astype(BF16)
        if i + 1 < nblk:
            m_next = scores(i + 1, i + 1, sel_next, m_next)
            sel, m = sel_next, m_next
    weighted_values(nblk - 1)


def _moba(q, k, v, kbar, *, batch, seq):
    t, d = q.shape
    hd = d // N_HEADS
    nblk = seq // MOBA_BLOCK
    kern = functools.partial(_moba_kernel, nblk=nblk, scale=hd ** -0.5)
    head_spec = pl.BlockSpec((seq, hd), lambda b, h: (b, h))
    return pl.pallas_call(
        kern,
        grid=(batch, N_HEADS),
        in_specs=[head_spec, head_spec, head_spec,
                  pl.BlockSpec((1, nblk, hd), lambda b, h: (b, 0, h))],
        out_specs=head_spec,
        out_shape=jax.ShapeDtypeStruct((t, d), BF16),
        scratch_shapes=[pltpu.VMEM((hd + BF16_SUBLANES, seq), BF16),
                        pltpu.VMEM((2, seq, MOBA_BLOCK), F32),
                        pltpu.VMEM((2, seq, MOBA_BLOCK), BF16)],
        compiler_params=_params(2),
        name="moba",
    )(q, k, v, kbar)


def _softplus2(z2):
    neg_abs = lax.bitcast_convert_type(
        lax.bitcast_convert_type(z2, jnp.uint32) | jnp.uint32(0x80000000), F32)
    return jnp.maximum(z2, 0.0) + jnp.log2(1.0 + jnp.exp2(neg_abs))


def _sb_kernel(q_ref, k_ref, v_ref, o_ref, vt_ref, z_ref, w_ref, acc_ref, tail_ref, *, nblk, scale):
    blk = SB_BLOCK
    key = lax.broadcasted_iota(jnp.int32, (blk, blk), 0)
    qry = lax.broadcasted_iota(jnp.int32, (blk, blk), 1)
    strict = key < qry
    later = (qry > key).astype(BF16)
    later2 = jnp.concatenate([later, later], axis=1)
    c2 = scale * LOG2E

    def sum_later(sp):
        hi = sp.astype(BF16)
        lo = (sp - hi.astype(F32)).astype(BF16)
        return _dot(later2, jnp.concatenate([hi, lo], axis=0))

    for j in range(nblk):
        vt_ref[:, j * blk:(j + 1) * blk] = v_ref[j * blk:(j + 1) * blk, :].astype(F32).T.astype(BF16)

    def logits(i, b0, b1):
        z_ref[i % 2, b0 * blk:b1 * blk, :] = _dot_t(k_ref[b0 * blk:b1 * blk, :], q_ref[i * blk:(i + 1) * blk, :]) * c2

    def block_weights(i, j, tail):
        z = z_ref[i % 2, j * blk:(j + 1) * blk, :]
        sp = _softplus2(z)
        if j == i:
            sp = jnp.where(strict, sp, 0.0)
        e = z - sp - sum_later(sp)
        w = jnp.exp2(e if tail is None else e - tail)
        if j == i:
            w = jnp.where(strict, w, 0.0)
        w_ref[i % 2, j * blk:(j + 1) * blk, :] = w.astype(BF16)
        if j == 0:
            return tail
        part = jnp.sum(sp, axis=0, keepdims=True)
        return part if tail is None else tail + part

    def near_values(i):
        b0 = max(i + 1 - SB_NEAR, 0)
        acc = _dot(vt_ref[:, b0 * blk:(i + 1) * blk], w_ref[i % 2, b0 * blk:(i + 1) * blk, :])
        if b0 > 0:
            acc_ref[i] = acc
        o_ref[i * blk:(i + 1) * blk, :] = acc.T.astype(o_ref.dtype)

    logits(0, 0, 1)
    least_tail = None
    for i in range(nblk):
        if i + 1 < nblk:
            logits(i + 1, max(i + 2 - SB_NEAR, 0), i + 2)
        if i > 0:
            near_values(i - 1)
        b0 = max(i + 1 - SB_NEAR, 0)
        tail = None
        for j in range(i, b0 - 1, -1):
            tail = block_weights(i, j, tail)
        if b0 > 0:
            tail_ref[i:i + 1, :] = tail
            least_tail = tail if least_tail is None else jnp.minimum(least_tail, tail)
    near_values(nblk - 1)

    if nblk > SB_NEAR:
        @pl.when(jnp.min(least_tail) < SB_DEAD_LOG2)
        def _():
            for i in range(SB_NEAR, nblk):
                b1 = i + 1 - SB_NEAR
                logits(i, 0, b1)
                tail = tail_ref[i:i + 1, :]
                for j in range(b1 - 1, -1, -1):
                    tail = block_weights(i, j, tail)
                acc = acc_ref[i] + _dot(vt_ref[:, 0:b1 * blk], w_ref[i % 2, 0:b1 * blk, :])
                o_ref[i * blk:(i + 1) * blk, :] = acc.T.astype(o_ref.dtype)


def _sb(q, kv, *, batch, seq):
    t, d = q.shape
    hd = d // N_HEADS
    nblk = seq // SB_BLOCK
    kern = functools.partial(_sb_kernel, nblk=nblk, scale=hd ** -0.5)
    return pl.pallas_call(
        kern,
        grid=(batch, N_HEADS),
        in_specs=[pl.BlockSpec((seq, hd), lambda b, h: (b, h)),
                  pl.BlockSpec((seq, hd), lambda b, h: (b, h)),
                  pl.BlockSpec((seq, hd), lambda b, h: (b, N_HEADS + h))],
        out_specs=pl.BlockSpec((seq, hd), lambda b, h: (b, h)),
        out_shape=jax.ShapeDtypeStruct((t, d), BF16),
        scratch_shapes=[pltpu.VMEM((hd, seq), BF16),
                        pltpu.VMEM((2, seq, SB_BLOCK), F32),
                        pltpu.VMEM((2, seq, SB_BLOCK), BF16),
                        pltpu.VMEM((nblk, hd, SB_BLOCK), F32),
                        pltpu.VMEM((nblk, SB_BLOCK), F32)],
        compiler_params=_params(2),
        name="stickbreak",
    )(q, kv, kv)


def kernel(x, c, w_ada, b_ada, ln_g, ln_b, w_ffn_in, w_ffn_out, w_qkv_a, w_q_b, w_kv_ada, b_kv_ada, w_kv_b, w_o):
    bsz, seq, d = x.shape
    depth = w_ada.shape[0]
    n_a = w_qkv_a.shape[0]
    alpha = (2.0 * depth) ** 0.25
    t = bsz * seq
    assert seq % MOBA_BLOCK == 0 and seq % SB_BLOCK == 0 and d % (N_HEADS * LANES) == 0

    mod_all = _ada(c, w_ada, b_ada).reshape(depth, bsz, 3 * N_SUB, d)
    kv_mod = _ada(c, w_kv_ada[None], b_kv_ada[None]).reshape(1, bsz, 2, d)

    w_in = w_ffn_in.astype(BF16)
    w_out = w_ffn_out.astype(BF16)
    w_qkv = w_qkv_a.astype(BF16)
    w_q = w_q_b.astype(BF16)
    w_kv = w_kv_b.astype(BF16)[None]
    w_ob = w_o.astype(BF16)
    tabs = _rotary_tables(seq, d // N_HEADS)

    xf = x.reshape(t, d)
    ffn = functools.partial(_ffn, mod_all=mod_all, w_in=w_in, w_out=w_out, ln_g=ln_g, ln_b=ln_b, seq=seq, alpha=alpha)
    kv_proj = (kv_mod, 0, (0, 1), w_kv, 0)
    kv = None
    if n_a == 0 and depth > 0:
        kv = _proj(xf, kv_mod, w_kv, mod_layer=0, w_layer=0, r_shift=0, r_scale=1, seq=seq, name="kv_proj")
    for l in range(depth):
        if l < n_a:
            xf = ffn(xf, layer=l, which=0)
            q, k, v, kbar = _qkv(xf, mod_all, w_qkv, tabs, layer=l, seq=seq)
            att = _moba(q, k, v, kbar.reshape(bsz, seq // MOBA_BLOCK, d), batch=bsz, seq=seq)
        else:
            xf, q = ffn(xf, layer=l, which=0, proj=(mod_all, l, (3, 4), w_q, l - n_a))
            att = _sb(q, kv, batch=bsz, seq=seq)
        if l + 1 == n_a and l + 1 < depth:
            xf, kv = ffn(xf, layer=l, which=1, mixer=(att, w_ob), proj=kv_proj)
        else:
            xf = ffn(xf, layer=l, which=1, mixer=(att, w_ob))
    return xf.reshape(bsz, seq, d)
```

```python
import functools

import jax
import jax.numpy as jnp
from jax import lax
from jax.experimental import pallas as pl
from jax.experimental.pallas import tpu as pltpu

N_HEADS = 8
ROT_FRACTION = 4
ROPE_THETA = 500000.0
MOBA_BLOCK = 256
MOBA_TOPK = 3
SB_BLOCK = 256
SB_NEAR = 2
SB_DEAD_LOG2 = 160.0
LN_EPS = 1e-5
NEG_INF = -1e30
N_SUB = 3
LANES = 128
BF16_SUBLANES = 16
LOG2E = 1.4426950408889634

F32 = jnp.float32
BF16 = jnp.bfloat16

VMEM_LIMIT = 56 * 1024 * 1024
FFN_TILE = 1024


def _params(n_axes):
    return pltpu.CompilerParams(
        dimension_semantics=("parallel",) * n_axes, vmem_limit_bytes=VMEM_LIMIT)


def _dot(a, b):
    return jnp.dot(a, b, preferred_element_type=F32)


def _dot_t(a, b):
    return lax.dot_general(a, b, (((1,), (1,)), ((), ())), preferred_element_type=F32)


def _layer_norm(r, g, b):
    mu = jnp.mean(r, axis=-1, keepdims=True)
    d = r - mu
    var = jnp.mean(d * d, axis=-1, keepdims=True)
    return d * lax.rsqrt(var + LN_EPS) * g + b


def _ada_kernel(c_ref, w_ref, b_ref, o_ref):
    c = c_ref[...]
    cond = (c * jax.nn.sigmoid(c)).astype(BF16)
    o_ref[...] = _dot(cond, w_ref[...].astype(BF16)) + b_ref[...]


def _ada(c, w, b, tn=1024):
    nl, d, n = w.shape
    bsz = c.shape[0]
    return pl.pallas_call(
        _ada_kernel,
        grid=(nl, n // tn),
        in_specs=[
            pl.BlockSpec((bsz, d), lambda l, j: (0, 0)),
            pl.BlockSpec((None, d, tn), lambda l, j: (l, 0, j)),
            pl.BlockSpec((None, 1, tn), lambda l, j: (l, 0, j)),
        ],
        out_specs=pl.BlockSpec((None, bsz, tn), lambda l, j: (l, 0, j)),
        out_shape=jax.ShapeDtypeStruct((nl, bsz, n), F32),
        compiler_params=_params(2),
        name="ada",
    )(c, w, b.reshape(nl, 1, n))


def _ffn_kernel(*refs, k0, sub, alpha, rows, mixer_out, proj_rows):
    it = iter(refs)
    x_ref, mod_ref = next(it), next(it)
    if mixer_out:
        att_ref, wmix_ref = next(it), next(it)
    wg_ref, wu_ref, wo_ref, lg_ref, lb_ref = (next(it) for _ in range(5))
    if proj_rows:
        pmod_ref, wproj_ref = next(it), next(it)
    o_ref = next(it)
    if proj_rows:
        p_ref = next(it)

    shift = mod_ref[0, k0:k0 + 1, :]
    scale = mod_ref[0, k0 + 1:k0 + 2, :]
    gate = mod_ref[0, k0 + 2:k0 + 3, :]
    def stream_in(r0):
        x = x_ref[r0:r0 + rows, :]
        if mixer_out:
            y = _dot(att_ref[r0:r0 + rows, :], wmix_ref[...])
            x = _layer_norm(alpha * x + (1.0 + mod_ref[0, 5:6, :]) * y, lg_ref[1:2, :], lb_ref[1:2, :])
        return x

    def project(r0, x):
        r_shift, r_scale = proj_rows
        hp = (x * (1.0 + pmod_ref[0, r_scale:r_scale + 1, :]) + pmod_ref[0, r_shift:r_shift + 1, :]).astype(BF16)
        p_ref[r0:r0 + rows, :] = _dot(hp, wproj_ref[...]).astype(p_ref.dtype)

    starts = list(range(0, x_ref.shape[0], rows))
    x_next = stream_in(starts[0])
    project_prev = None
    for n, r0 in enumerate(starts):
        x = x_next
        if n + 1 < len(starts):
            x_next = stream_in(starts[n + 1])
        h = (x * (1.0 + scale) + shift).astype(BF16)
        g = _dot(h, wg_ref[...])
        u = _dot(h, wu_ref[...])
        if project_prev is not None:
            project_prev()
            project_prev = None
        a = (g * jax.nn.sigmoid(g) * u).astype(BF16)
        y = _dot(a, wo_ref[...])
        r = alpha * x + (0.5 * (1.0 + gate)) * y
        x = _layer_norm(r, lg_ref[sub:sub + 1, :], lb_ref[sub:sub + 1, :])
        o_ref[r0:r0 + rows, :] = x
        if proj_rows:
            project_prev = functools.partial(project, r0, x)
    if project_prev is not None:
        project_prev()


def _ffn_tile(d, f, seq, rows, *, with_mixer, n_proj):
    weights = 2 * (3 * d * f + (d * d if with_mixer else 0) + d * n_proj)
    per_token = 2 * (4 * d + 4 * d + (2 * d if with_mixer else 0) + 2 * n_proj)
    temporaries = rows * (2 * f * 4 + f * 2 + 4 * d * 4)
    tm = FFN_TILE
    while tm > rows and (seq % tm or weights + per_token * tm + temporaries > VMEM_LIMIT):
        tm //= 2
    return tm


def _ffn(x, mod_all, w_in, w_out, ln_g, ln_b, *, layer, which, seq, alpha, mixer=None, proj=None, rows=256):
    t, d = x.shape
    f = w_out.shape[2]
    k0 = 0 if which == 0 else 6
    sub = 0 if which == 0 else 2
    n_proj = 0 if proj is None else proj[3].shape[2]
    tm = _ffn_tile(d, f, seq, rows, with_mixer=mixer is not None, n_proj=n_proj)
    per_seq = seq // tm
    tok_spec = pl.BlockSpec((tm, d), lambda i: (i, 0))
    ln_spec = pl.BlockSpec((None, N_SUB, d), lambda i: (layer, 0, 0))
    args = [x, mod_all]
    in_specs = [tok_spec, pl.BlockSpec((None, 1, 3 * N_SUB, d), lambda i: (layer, i // per_seq, 0, 0))]
    if mixer is not None:
        att, w_mix = mixer
        args += [att, w_mix]
        in_specs += [tok_spec, pl.BlockSpec((None, d, d), lambda i: (layer, 0, 0))]
    args += [w_in, w_in, w_out, ln_g, ln_b]
    in_specs += [
        pl.BlockSpec((None, None, d, f), lambda i: (layer, which, 0, 0)),
        pl.BlockSpec((None, None, d, f), lambda i: (layer, which, 0, 1)),
        pl.BlockSpec((None, None, f, d), lambda i: (layer, which, 0, 0)),
        ln_spec, ln_spec,
    ]
    out_specs = [tok_spec]
    out_shape = [jax.ShapeDtypeStruct((t, d), F32)]
    proj_rows = None
    if proj is not None:
        pmod, mod_layer, proj_rows, w_proj, w_layer = proj
        n = w_proj.shape[2]
        args += [pmod, w_proj]
        in_specs += [pl.BlockSpec((None, 1, pmod.shape[2], d), lambda i: (mod_layer, i // per_seq, 0, 0)),
                     pl.BlockSpec((None, d, n), lambda i: (w_layer, 0, 0))]
        out_specs.append(pl.BlockSpec((tm, n), lambda i: (i, 0)))
        out_shape.append(jax.ShapeDtypeStruct((t, n), BF16))
    kern = functools.partial(_ffn_kernel, k0=k0, sub=sub, alpha=alpha, rows=rows,
                             mixer_out=mixer is not None, proj_rows=proj_rows)
    outs = pl.pallas_call(
        kern,
        grid=(t // tm,),
        in_specs=in_specs,
        out_specs=out_specs,
        out_shape=out_shape,
        compiler_params=_params(1),
        name=f"ffn_l{layer}_{which}",
    )(*args)
    return outs if proj is not None else outs[0]


def _rotary_tables(seq, head_dim):
    rot = head_dim // ROT_FRACTION
    half = rot // 2
    inv_freq = jnp.power(ROPE_THETA, -jnp.arange(half, dtype=F32) * 2.0 / rot)
    ang = jnp.arange(seq, dtype=F32)[:, None] * inv_freq[None, :]
    cos, sin = jnp.cos(ang), jnp.sin(ang)
    ones = jnp.ones((seq, head_dim - rot), F32)
    zeros = jnp.zeros((seq, head_dim - rot), F32)
    zh = jnp.zeros((seq, half), F32)
    c_tab = jnp.concatenate([cos, cos, ones], axis=1)
    up_tab = jnp.concatenate([zh, sin, zeros], axis=1)
    dn_tab = jnp.concatenate([-sin, zh, zeros], axis=1)
    return c_tab, up_tab, dn_tab


def _qkv_kernel(x_ref, mod_ref, w_ref, c_ref, up_ref, dn_ref, q_ref, k_ref, v_ref, kbar_ref, *, half):
    x = x_ref[...]
    d = x.shape[1]
    hd = c_ref.shape[1]
    shift = mod_ref[0, 3:4, :]
    scale = mod_ref[0, 4:5, :]
    h = (x * (1.0 + scale) + shift).astype(BF16)
    qkv = _dot(h, w_ref[...])
    c_tab, up_tab, dn_tab = c_ref[...], up_ref[...], dn_ref[...]

    def rotate(t):
        parts = []
        for hh in range(d // hd):
            th = t[:, hh * hd:(hh + 1) * hd]
            parts.append(th * c_tab
                         + pltpu.roll(th, half, 1) * up_tab
                         + pltpu.roll(th, hd - half, 1) * dn_tab)
        return jnp.concatenate(parts, axis=1)

    q = rotate(qkv[:, :d])
    k = rotate(qkv[:, d:2 * d])
    q_ref[...] = q.astype(q_ref.dtype)
    k_ref[...] = k.astype(k_ref.dtype)
    v_ref[...] = qkv[:, 2 * d:].astype(v_ref.dtype)
    for r in range(k.shape[0] // MOBA_BLOCK):
        kbar_ref[0, r:r + 1, :] = jnp.mean(k[r * MOBA_BLOCK:(r + 1) * MOBA_BLOCK], axis=0, keepdims=True)


def _qkv(x, mod_all, w, tabs, *, layer, seq, tm=512):
    t, d = x.shape
    hd = d // N_HEADS
    per_seq = seq // tm
    nb = tm // MOBA_BLOCK
    kern = functools.partial(_qkv_kernel, half=hd // ROT_FRACTION // 2)
    tab_spec = pl.BlockSpec((tm, hd), lambda i: (i % per_seq, 0))
    tok_spec = pl.BlockSpec((tm, d), lambda i: (i, 0))
    return pl.pallas_call(
        kern,
        grid=(t // tm,),
        in_specs=[
            tok_spec,
            pl.BlockSpec((None, 1, 3 * N_SUB, d), lambda i: (layer, i // per_seq, 0, 0)),
            pl.BlockSpec((None, d, 3 * d), lambda i: (layer, 0, 0)),
            tab_spec, tab_spec, tab_spec,
        ],
        out_specs=[tok_spec, tok_spec, tok_spec,
                   pl.BlockSpec((1, nb, d), lambda i: (i, 0, 0))],
        out_shape=[jax.ShapeDtypeStruct((t, d), BF16)] * 3
        + [jax.ShapeDtypeStruct((t // tm, nb, d), F32)],
        compiler_params=_params(1),
        name=f"qkv_l{layer}",
    )(x, mod_all, w, *tabs)


def _proj_kernel(x_ref, mod_ref, w_ref, o_ref, *, r_shift, r_scale):
    x = x_ref[...]
    shift = mod_ref[0, r_shift:r_shift + 1, :]
    scale = mod_ref[0, r_scale:r_scale + 1, :]
    h = (x * (1.0 + scale) + shift).astype(BF16)
    o_ref[...] = _dot(h, w_ref[...]).astype(o_ref.dtype)


def _proj(x, mod, w, *, mod_layer, w_layer, r_shift, r_scale, seq, tm=512, name="proj"):
    t, d = x.shape
    n = w.shape[2]
    rows = mod.shape[2]
    per_seq = seq // tm
    kern = functools.partial(_proj_kernel, r_shift=r_shift, r_scale=r_scale)
    return pl.pallas_call(
        kern,
        grid=(t // tm,),
        in_specs=[
            pl.BlockSpec((tm, d), lambda i: (i, 0)),
            pl.BlockSpec((None, 1, rows, d), lambda i: (mod_layer, i // per_seq, 0, 0)),
            pl.BlockSpec((None, d, n), lambda i: (w_layer, 0, 0)),
        ],
        out_specs=pl.BlockSpec((tm, n), lambda i: (i, 0)),
        out_shape=jax.ShapeDtypeStruct((t, n), BF16),
        compiler_params=_params(1),
        name=name,
    )(x, mod, w)


def _moba_kernel(q_ref, k_ref, v_ref, kbar_ref, o_ref, vt_ref, s_ref, p_ref, *, nblk, scale):
    blk = MOBA_BLOCK
    hd = q_ref.shape[1]
    grows = -(-nblk // BF16_SUBLANES) * BF16_SUBLANES
    kb = jnp.concatenate([kbar_ref[0], jnp.zeros((grows - nblk, hd), F32)], axis=0)
    kb_hi = kb.astype(BF16)
    kb_lo = (kb - kb_hi.astype(F32)).astype(BF16)
    key = lax.broadcasted_iota(jnp.int32, (blk, blk), 0)
    qry = lax.broadcasted_iota(jnp.int32, (blk, blk), 1)
    causal = key <= qry
    grow = lax.broadcasted_iota(jnp.int32, (grows, blk), 0).astype(F32)
    c2 = scale * LOG2E

    for j in range(nblk):
        vt_ref[0:hd, j * blk:(j + 1) * blk] = v_ref[j * blk:(j + 1) * blk, :].astype(F32).T.astype(BF16)
    vt_ref[hd:, :] = jnp.ones((vt_ref.shape[0] - hd, vt_ref.shape[1]), BF16)

    def select_blocks(i):
        if i <= MOBA_TOPK:
            return None
        qi = q_ref[i * blk:(i + 1) * blk, :]
        gate = _dot_t(kb_hi, qi) + _dot_t(kb_lo, qi)
        gate = jnp.where(grow < float(i), gate, -jnp.inf)
        sel = jnp.full((grows, blk), NEG_INF, F32)
        for _ in range(MOBA_TOPK):
            top = jnp.max(gate, axis=0, keepdims=True)
            first = jnp.min(jnp.where(gate == top, grow, float(grows)), axis=0, keepdims=True)
            hit = grow == first
            sel = jnp.where(hit, 0.0, sel)
            gate = jnp.where(hit, -jnp.inf, gate)
        return sel

    def scores(i, j, sel, m):
        s = _dot_t(k_ref[j * blk:(j + 1) * blk, :], q_ref[i * blk:(i + 1) * blk, :]) * c2
        s_ref[i % 2, j * blk:(j + 1) * blk, :] = s
        if j == i:
            s = jnp.where(causal, s, NEG_INF)
        elif sel is not None:
            s = s + sel[j:j + 1, :]
        top = jnp.max(s, axis=0, keepdims=True)
        return top if m is None else jnp.maximum(m, top)

    def weighted_values(i):
        acc = _dot(vt_ref[:, 0:(i + 1) * blk], p_ref[i % 2, 0:(i + 1) * blk, :])
        o_ref[i * blk:(i + 1) * blk, :] = (acc[:hd, :] * (1.0 / acc[hd:hd + 1, :])).T.astype(o_ref.dtype)

    sel, m = None, scores(0, 0, None, None)
    for i in range(nblk):
        if i + 1 < nblk:
            sel_next = select_blocks(i + 1)
        if i > 0:
            weighted_values(i - 1)
        m_next = None
        for j in range(i + 1):
            if i + 1 < nblk:
                m_next = scores(i + 1, j, sel_next, m_next)
            s = s_ref[i % 2, j * blk:(j + 1) * blk, :]
            if j == i:
                p = jnp.exp2(jnp.where(causal, s, NEG_INF) - m)
            else:
                p = jnp.exp2(s + ((-m) if sel is None else (sel[j:j + 1, :] - m)))
            p_ref[i % 2, j * blk:(j + 1) * blk, :] = p.astype(BF16)
        if i + 1 < nblk:
            m_next = scores(i + 1, i + 1, sel_next, m_next)
            sel, m = sel_next, m_next
    weighted_values(nblk - 1)


def _moba(q, k, v, kbar, *, batch, seq):
    t, d = q.shape
    hd = d // N_HEADS
    nblk = seq // MOBA_BLOCK
    kern = functools.partial(_moba_kernel, nblk=nblk, scale=hd ** -0.5)
    head_spec = pl.BlockSpec((seq, hd), lambda b, h: (b, h))
    return pl.pallas_call(
        kern,
        grid=(batch, N_HEADS),
        in_specs=[head_spec, head_spec, head_spec,
                  pl.BlockSpec((1, nblk, hd), lambda b, h: (b, 0, h))],
        out_specs=head_spec,
        out_shape=jax.ShapeDtypeStruct((t, d), BF16),
        scratch_shapes=[pltpu.VMEM((hd + BF16_SUBLANES, seq), BF16),
                        pltpu.VMEM((2, seq, MOBA_BLOCK), F32),
                        pltpu.VMEM((2, seq, MOBA_BLOCK), BF16)],
        compiler_params=_params(2),
        name="moba",
    )(q, k, v, kbar)


def _softplus2(z2):
    neg_abs = lax.bitcast_convert_type(
        lax.bitcast_convert_type(z2, jnp.uint32) | jnp.uint32(0x80000000), F32)
    return jnp.maximum(z2, 0.0) + jnp.log2(1.0 + jnp.exp2(neg_abs))


def _sb_kernel(q_ref, k_ref, v_ref, o_ref, vt_ref, z_ref, w_ref, acc_ref, tail_ref, *, nblk, scale):
    blk = SB_BLOCK
    key = lax.broadcasted_iota(jnp.int32, (blk, blk), 0)
    qry = lax.broadcasted_iota(jnp.int32, (blk, blk), 1)
    strict = key < qry
    later = (qry > key).astype(BF16)
    later2 = jnp.concatenate([later, later], axis=1)
    c2 = scale * LOG2E

    def sum_later(sp):
        hi = sp.astype(BF16)
        lo = (sp - hi.astype(F32)).astype(BF16)
        return _dot(later2, jnp.concatenate([hi, lo], axis=0))

    for j in range(nblk):
        vt_ref[:, j * blk:(j + 1) * blk] = v_ref[j * blk:(j + 1) * blk, :].astype(F32).T.astype(BF16)

    def logits(i, b0, b1):
        z_ref[i % 2, b0 * blk:b1 * blk, :] = _dot_t(k_ref[b0 * blk:b1 * blk, :], q_ref[i * blk:(i + 1) * blk, :]) * c2

    def block_weights(i, j, tail):
        z = z_ref[i % 2, j * blk:(j + 1) * blk, :]
        sp = _softplus2(z)
        if j == i:
            sp = jnp.where(strict, sp, 0.0)
        e = z - sp - sum_later(sp)
        w = jnp.exp2(e if tail is None else e - tail)
        if j == i:
            w = jnp.where(strict, w, 0.0)
        w_ref[i % 2, j * blk:(j + 1) * blk, :] = w.astype(BF16)
        if j == 0:
            return tail
        part = jnp.sum(sp, axis=0, keepdims=True)
        return part if tail is None else tail + part

    def near_values(i):
        b0 = max(i + 1 - SB_NEAR, 0)
        acc = _dot(vt_ref[:, b0 * blk:(i + 1) * blk], w_ref[i % 2, b0 * blk:(i + 1) * blk, :])
        if b0 > 0:
            acc_ref[i] = acc
        o_ref[i * blk:(i + 1) * blk, :] = acc.T.astype(o_ref.dtype)

    logits(0, 0, 1)
    least_tail = None
    for i in range(nblk):
        if i + 1 < nblk:
            logits(i + 1, max(i + 2 - SB_NEAR, 0), i + 2)
        if i > 0:
            near_values(i - 1)
        b0 = max(i + 1 - SB_NEAR, 0)
        tail = None
        for j in range(i, b0 - 1, -1):
            tail = block_weights(i, j, tail)
        if b0 > 0:
            tail_ref[i:i + 1, :] = tail
            least_tail = tail if least_tail is None else jnp.minimum(least_tail, tail)
    near_values(nblk - 1)

    if nblk > SB_NEAR:
        @pl.when(jnp.min(least_tail) < SB_DEAD_LOG2)
        def _():
            for i in range(SB_NEAR, nblk):
                b1 = i + 1 - SB_NEAR
                logits(i, 0, b1)
                tail = tail_ref[i:i + 1, :]
                for j in range(b1 - 1, -1, -1):
                    tail = block_weights(i, j, tail)
                acc = acc_ref[i] + _dot(vt_ref[:, 0:b1 * blk], w_ref[i % 2, 0:b1 * blk, :])
                o_ref[i * blk:(i + 1) * blk, :] = acc.T.astype(o_ref.dtype)


def _sb(q, kv, *, batch, seq):
    t, d = q.shape
    hd = d // N_HEADS
    nblk = seq // SB_BLOCK
    kern = functools.partial(_sb_kernel, nblk=nblk, scale=hd ** -0.5)
    return pl.pallas_call(
        kern,
        grid=(batch, N_HEADS),
        in_specs=[pl.BlockSpec((seq, hd), lambda b, h: (b, h)),
                  pl.BlockSpec((seq, hd), lambda b, h: (b, h)),
                  pl.BlockSpec((seq, hd), lambda b, h: (b, N_HEADS + h))],
        out_specs=pl.BlockSpec((seq, hd), lambda b, h: (b, h)),
        out_shape=jax.ShapeDtypeStruct((t, d), BF16),
        scratch_shapes=[pltpu.VMEM((hd, seq), BF16),
                        pltpu.VMEM((2, seq, SB_BLOCK), F32),
                        pltpu.VMEM((2, seq, SB_BLOCK), BF16),
                        pltpu.VMEM((nblk, hd, SB_BLOCK), F32),
                        pltpu.VMEM((nblk, SB_BLOCK), F32)],
        compiler_params=_params(2),
        name="stickbreak",
    )(q, kv, kv)


def kernel(x, c, w_ada, b_ada, ln_g, ln_b, w_ffn_in, w_ffn_out, w_qkv_a, w_q_b, w_kv_ada, b_kv_ada, w_kv_b, w_o):
    bsz, seq, d = x.shape
    depth = w_ada.shape[0]
    n_a = w_qkv_a.shape[0]
    alpha = (2.0 * depth) ** 0.25
    t = bsz * seq
    assert seq % MOBA_BLOCK == 0 and seq % SB_BLOCK == 0 and d % (N_HEADS * LANES) == 0

    mod_all = _ada(c, w_ada, b_ada).reshape(depth, bsz, 3 * N_SUB, d)
    kv_mod = _ada(c, w_kv_ada[None], b_kv_ada[None]).reshape(1, bsz, 2, d)

    w_in = w_ffn_in.astype(BF16)
    w_out = w_ffn_out.astype(BF16)
    w_qkv = w_qkv_a.astype(BF16)
    w_q = w_q_b.astype(BF16)
    w_kv = w_kv_b.astype(BF16)[None]
    w_ob = w_o.astype(BF16)
    tabs = _rotary_tables(seq, d // N_HEADS)

    xf = x.reshape(t, d)
    ffn = functools.partial(_ffn, mod_all=mod_all, w_in=w_in, w_out=w_out, ln_g=ln_g, ln_b=ln_b, seq=seq, alpha=alpha)
    kv_proj = (kv_mod, 0, (0, 1), w_kv, 0)
    kv = None
    if n_a == 0 and depth > 0:
        kv = _proj(xf, kv_mod, w_kv, mod_layer=0, w_layer=0, r_shift=0, r_scale=1, seq=seq, name="kv_proj")
    for l in range(depth):
        if l < n_a:
            xf = ffn(xf, layer=l, which=0)
            q, k, v, kbar = _qkv(xf, mod_all, w_qkv, tabs, layer=l, seq=seq)
            att = _moba(q, k, v, kbar.reshape(bsz, seq // MOBA_BLOCK, d), batch=bsz, seq=seq)
        else:
            xf, q = ffn(xf, layer=l, which=0, proj=(mod_all, l, (3, 4), w_q, l - n_a))
            att = _sb(q, kv, batch=bsz, seq=seq)
        if l + 1 == n_a and l + 1 < depth:
            xf, kv = ffn(xf, layer=l, which=1, mixer=(att, w_ob), proj=kv_proj)
        else:
            xf = ffn(xf, layer=l, which=1, mixer=(att, w_ob))
    return xf.reshape(bsz, seq, d)
```
